```python
import math, functools
import jax, jax.numpy as jnp
from jax import lax
import numpy as np

D_MODEL = 1024
BATCH = 4
SEQ = 4096
DEPTH = 1
DEC_BATCH = 32
DEC_SEQ = 1
PAST_LEN = 8192
PAGE_SIZE = 128

N_META = 16
D_MIX = D_MODEL
D_ATT = D_MIX // 2
D_RNN = D_MIX - D_ATT
HEAD_DIM = 64
N_ATT_HEADS = D_ATT // HEAD_DIM
N_RNN_BLOCKS = 8
RNN_BLOCK = D_RNN // N_RNN_BLOCKS
RNN_CONV = 4
RG_C = 8.0
D_FF = 11 * D_MODEL // 4
FFN_CONV = 3
Q_BLOCK = 128
EPS = 1e-6
ATT_SCALE = HEAD_DIM ** -0.5
FORGET_BIAS_LO = 2.0
FORGET_BIAS_HI = 6.0
D_IN = 3 * D_ATT + N_ATT_HEADS + 2 * D_RNN
SPLIT_IDX = (D_ATT, 2 * D_ATT, 3 * D_ATT, 3 * D_ATT + N_ATT_HEADS, 3 * D_ATT + N_ATT_HEADS + D_RNN)

kernel_name = 'fox_rglru_hymba_decode_step'


def rmsnorm(x, g):
    xf = x.astype(jnp.float32)
    y = xf * lax.rsqrt(jnp.mean(xf * xf, axis=-1, keepdims=True) + EPS) * g.astype(jnp.float32)
    return y.astype(x.dtype)


def causal_dwconv(x, buf, w, b):
    t = x.shape[1]
    width = w.shape[0]
    xp = jnp.concatenate([buf.astype(x.dtype), x], axis=1)
    y = b + sum(w[j] * xp[:, j:j + t] for j in range(width))
    return y, xp[:, t:]


def linear_recurrence(a, u, h0):
    def combine(left, right):
        a_l, u_l = left
        a_r, u_r = right
        return a_l * a_r, a_r * u_l + u_r
    a_cum, u_cum = lax.associative_scan(combine, (a, u), axis=1)
    return a_cum * h0[:, None, :] + u_cum


def forget_attend(q, cq, qpos, k, v, ck, kpos):
    s = jnp.einsum('bqhd,bkhd->bhqk', q, k).astype(jnp.float32) * ATT_SCALE
    s = s + jnp.swapaxes(cq, 1, 2)[:, :, :, None] - jnp.swapaxes(ck, 1, 2)[:, :, None, :]
    s = jnp.where(kpos[None, :] <= qpos[:, None], s, -jnp.inf)
    p = jax.nn.softmax(s, axis=-1)
    return jnp.einsum('bhqk,bkhd->bqhd', p.astype(v.dtype), v)


def prompt_attention(q, k, v, logf):
    b, t = q.shape[:2]
    n_real = t - N_META
    nb = n_real // Q_BLOCK
    c = jnp.cumsum(logf.astype(jnp.float32), axis=1)
    pos = jnp.arange(t)
    o_meta = forget_attend(q[:, :N_META], c[:, :N_META], pos[:N_META],
                           k[:, :N_META], v[:, :N_META], c[:, :N_META], pos[:N_META])
    qb = q[:, N_META:].reshape(b, nb, Q_BLOCK, N_ATT_HEADS, HEAD_DIM).swapaxes(0, 1)
    cb = c[:, N_META:].reshape(b, nb, Q_BLOCK, N_ATT_HEADS).swapaxes(0, 1)
    pb = pos[N_META:].reshape(nb, Q_BLOCK)
    ob = lax.map(lambda blk: forget_attend(blk[0], blk[1], blk[2], k, v, c, pos), (qb, cb, pb))
    o_real = ob.swapaxes(0, 1).reshape(b, n_real, N_ATT_HEADS, HEAD_DIM)
    return jnp.concatenate([o_meta, o_real], axis=1)


def sample_attention(q, k, v, logf, k_past, v_past, logf_past):
    p_len = k_past.shape[1]
    s_len = q.shape[1]
    k_all = jnp.concatenate([k_past.astype(k.dtype), k], axis=1)
    v_all = jnp.concatenate([v_past.astype(v.dtype), v], axis=1)
    c = jnp.cumsum(jnp.concatenate([logf_past.astype(jnp.float32), logf.astype(jnp.float32)], axis=1), axis=1)
    kpos = jnp.arange(p_len + s_len)
    qpos = p_len + jnp.arange(s_len)
    return forget_attend(q, c[:, p_len:], qpos, k_all, v_all, c, kpos)


def rg_lru_branch(xr, gr, conv_buf, h0, lp):
    xc, conv_new = causal_dwconv(xr, conv_buf, lp['rnn_conv_w'], lp['rnn_conv_b'])
    b, t = xc.shape[:2]
    xb = xc.reshape(b, t, N_RNN_BLOCKS, RNN_BLOCK)
    gate_a = jnp.einsum('btnc,ncd->btnd', xb, lp['w_rg_a']).reshape(b, t, D_RNN) + lp['b_rg_a']
    gate_x = jnp.einsum('btnc,ncd->btnd', xb, lp['w_rg_x']).reshape(b, t, D_RNN) + lp['b_rg_x']
    r = jax.nn.sigmoid(gate_a.astype(jnp.float32))
    i = jax.nn.sigmoid(gate_x.astype(jnp.float32))
    log_a = -RG_C * r * jax.nn.softplus(-lp['rg_lambda'].astype(jnp.float32))
    a = jnp.exp(log_a)
    u = jnp.sqrt(-jnp.expm1(2.0 * log_a)) * (i * xc.astype(jnp.float32))
    h = linear_recurrence(a, u, h0.astype(jnp.float32))
    y = h * jax.nn.gelu(gr.astype(jnp.float32))
    return y.astype(xr.dtype), h[:, -1], conv_new


def gated_conv_ffn(hn, buf, lp):
    g_pre, val = jnp.split(hn @ lp['w_ffn_up'], 2, axis=-1)
    g, buf_new = causal_dwconv(g_pre, buf, lp['ffn_conv_w'], lp['ffn_conv_b'])
    return (jax.nn.gelu(g) * val) @ lp['w_ffn_down'], buf_new


def layer_forward(x, attention, rnn_conv_buf, rnn_h0, ffn_buf, lp):
    b, t = x.shape[:2]
    hn = rmsnorm(x, lp['norm_mix'])
    q, k, v, f_logit, xr, gr = jnp.split(hn @ lp['w_in'], SPLIT_IDX, axis=-1)
    q = q.reshape(b, t, N_ATT_HEADS, HEAD_DIM)
    k = k.reshape(b, t, N_ATT_HEADS, HEAD_DIM)
    v = v.reshape(b, t, N_ATT_HEADS, HEAD_DIM)
    logf = jax.nn.log_sigmoid(f_logit.astype(jnp.float32) + lp['b_forget'].astype(jnp.float32))
    att = attention(q, k, v, logf).reshape(b, t, D_ATT)
    rnn, h_last, rnn_conv_new = rg_lru_branch(xr, gr, rnn_conv_buf, rnn_h0, lp)
    merged = jnp.concatenate([rmsnorm(att, lp['norm_att_out']), rmsnorm(rnn, lp['norm_rnn_out'])], axis=-1)
    x = x + merged @ lp['w_out']
    f, ffn_new = gated_conv_ffn(rmsnorm(x, lp['norm_ffn']), ffn_buf, lp)
    x = x + f
    return x, (k, v, logf, h_last, rnn_conv_new, ffn_new)


def setup_inputs(seed: int = 0) -> dict:
    key = jax.random.key(seed)
    ks = jax.random.split(key, 32)
    f32 = jnp.float32
    n_pages = PAST_LEN // PAGE_SIZE
    n_used = DEC_BATCH * n_pages
    n_pool = n_used + max(1, n_used // 4)

    def nrm(k, shape, scale=1.0):
        return scale * jax.random.normal(k, shape, f32)

    head_bias = jnp.linspace(FORGET_BIAS_LO, FORGET_BIAS_HI, N_ATT_HEADS, dtype=f32)
    u = jax.random.uniform(ks[19], (DEPTH, D_RNN), f32, 0.9, 0.999)
    s = u ** (1.0 / RG_C)
    return {
        'x_prompt': nrm(ks[0], (BATCH, SEQ, D_MODEL)),
        'x_sample': nrm(ks[1], (DEC_BATCH, DEC_SEQ, D_MODEL)),
        'cache_k': nrm(ks[2], (DEPTH, n_pool, PAGE_SIZE, N_ATT_HEADS, HEAD_DIM)),
        'cache_v': nrm(ks[3], (DEPTH, n_pool, PAGE_SIZE, N_ATT_HEADS, HEAD_DIM)),
        'cache_logf': jax.nn.log_sigmoid(head_bias + nrm(ks[4], (DEPTH, n_pool, PAGE_SIZE, N_ATT_HEADS))),
        'state_rnn_h': nrm(ks[5], (DEPTH, DEC_BATCH, D_RNN), 0.5),
        'state_rnn_conv': nrm(ks[6], (DEPTH, DEC_BATCH, RNN_CONV - 1, D_RNN)),
        'state_ffn_conv': nrm(ks[7], (DEPTH, DEC_BATCH, FFN_CONV - 1, D_FF)),
        'page_table': jax.random.permutation(ks[8], n_pool)[:n_used].reshape(DEC_BATCH, n_pages).astype(jnp.int32),
        'meta_tokens': nrm(ks[9], (N_META, D_MODEL)),
        'norm_mix': 1.0 + nrm(ks[10], (DEPTH, D_MODEL), 0.02),
        'w_in': nrm(ks[11], (DEPTH, D_MODEL, D_IN), D_MODEL ** -0.5),
        'b_forget': head_bias + nrm(ks[12], (DEPTH, N_ATT_HEADS), 0.1),
        'rnn_conv_w': nrm(ks[13], (DEPTH, RNN_CONV, D_RNN), RNN_CONV ** -0.5),
        'rnn_conv_b': nrm(ks[14], (DEPTH, D_RNN), 0.02),
        'w_rg_a': nrm(ks[15], (DEPTH, N_RNN_BLOCKS, RNN_BLOCK, RNN_BLOCK), RNN_BLOCK ** -0.5),
        'b_rg_a': nrm(ks[16], (DEPTH, D_RNN), 0.02),
        'w_rg_x': nrm(ks[17], (DEPTH, N_RNN_BLOCKS, RNN_BLOCK, RNN_BLOCK), RNN_BLOCK ** -0.5),
        'b_rg_x': nrm(ks[18], (DEPTH, D_RNN), 0.02),
        'rg_lambda': jnp.log(s) - jnp.log1p(-s),
        'norm_att_out': 1.0 + nrm(ks[20], (DEPTH, D_ATT), 0.02),
        'norm_rnn_out': 1.0 + nrm(ks[21], (DEPTH, D_RNN), 0.02),
        'w_out': nrm(ks[22], (DEPTH, D_MIX, D_MODEL), D_MIX ** -0.5),
        'norm_ffn': 1.0 + nrm(ks[23], (DEPTH, D_MODEL), 0.02),
        'w_ffn_up': nrm(ks[24], (DEPTH, D_MODEL, 2 * D_FF), D_MODEL ** -0.5),
        'ffn_conv_w': nrm(ks[25], (DEPTH, FFN_CONV, D_FF), FFN_CONV ** -0.5),
        'ffn_conv_b': nrm(ks[26], (DEPTH, D_FF), 0.02),
        'w_ffn_down': nrm(ks[27], (DEPTH, D_FF, D_MODEL), D_FF ** -0.5),
        'norm_final': 1.0 + nrm(ks[28], (D_MODEL,), 0.02),
    }


def reference(x_prompt, x_sample, cache_k, cache_v, cache_logf, state_rnn_h, state_rnn_conv,
              state_ffn_conv, page_table, meta_tokens, norm_mix, w_in, b_forget, rnn_conv_w,
              rnn_conv_b, w_rg_a, b_rg_a, w_rg_x, b_rg_x, rg_lambda, norm_att_out, norm_rnn_out,
              w_out, norm_ffn, w_ffn_up, ffn_conv_w, ffn_conv_b, w_ffn_down, norm_final):
    b = x_prompt.shape[0]
    db = x_sample.shape[0]
    xp = jnp.concatenate([jnp.broadcast_to(meta_tokens.astype(x_prompt.dtype)[None], (b, N_META, D_MODEL)),
                          x_prompt], axis=1)
    xs = x_sample
    kp_l, vp_l, fp_l, hp_l, rcp_l, fcp_l = [], [], [], [], [], []
    ks_l, vs_l, fs_l, hs_l, rcs_l, fcs_l = [], [], [], [], [], []
    for l in range(DEPTH):
        lp = {
            'norm_mix': norm_mix[l], 'w_in': w_in[l], 'b_forget': b_forget[l],
            'rnn_conv_w': rnn_conv_w[l], 'rnn_conv_b': rnn_conv_b[l],
            'w_rg_a': w_rg_a[l], 'b_rg_a': b_rg_a[l], 'w_rg_x': w_rg_x[l], 'b_rg_x': b_rg_x[l],
            'rg_lambda': rg_lambda[l], 'norm_att_out': norm_att_out[l], 'norm_rnn_out': norm_rnn_out[l],
            'w_out': w_out[l], 'norm_ffn': norm_ffn[l], 'w_ffn_up': w_ffn_up[l],
            'ffn_conv_w': ffn_conv_w[l], 'ffn_conv_b': ffn_conv_b[l], 'w_ffn_down': w_ffn_down[l],
        }
        xp, (kp, vp, fp, hp, rcp, fcp) = layer_forward(
            xp, prompt_attention,
            jnp.zeros((b, RNN_CONV - 1, D_RNN), xp.dtype),
            jnp.zeros((b, D_RNN), jnp.float32),
            jnp.zeros((b, FFN_CONV - 1, D_FF), xp.dtype), lp)
        k_past = cache_k[l][page_table].reshape(db, -1, N_ATT_HEADS, HEAD_DIM)
        v_past = cache_v[l][page_table].reshape(db, -1, N_ATT_HEADS, HEAD_DIM)
        f_past = cache_logf[l][page_table].reshape(db, -1, N_ATT_HEADS)
        att_fn = functools.partial(sample_attention, k_past=k_past, v_past=v_past, logf_past=f_past)
        xs, (kn, vn, fn, hn, rcn, fcn) = layer_forward(
            xs, att_fn, state_rnn_conv[l], state_rnn_h[l], state_ffn_conv[l], lp)
        kp_l.append(kp); vp_l.append(vp); fp_l.append(fp); hp_l.append(hp); rcp_l.append(rcp); fcp_l.append(fcp)
        ks_l.append(kn); vs_l.append(vn); fs_l.append(fn); hs_l.append(hn); rcs_l.append(rcn); fcs_l.append(fcn)
    y_prompt = rmsnorm(xp, norm_final)[:, N_META:]
    y_sample = rmsnorm(xs, norm_final)
    return (y_prompt, y_sample,
            jnp.stack(kp_l), jnp.stack(vp_l), jnp.stack(fp_l), jnp.stack(hp_l), jnp.stack(rcp_l), jnp.stack(fcp_l),
            jnp.stack(ks_l), jnp.stack(vs_l), jnp.stack(fs_l), jnp.stack(hs_l), jnp.stack(rcs_l), jnp.stack(fcs_l))
```

```python
import functools

import jax
import jax.numpy as jnp
from jax import lax
from jax.experimental import pallas as pl
from jax.experimental.pallas import tpu as pltpu

D_MODEL = 1024
N_META = 16
D_ATT = 512
D_RNN = 512
HEAD_DIM = 64
N_HEADS = 8
N_RNN_BLOCKS = 8
RG_C = 8.0
D_FF = 2816
EPS = 1e-6
ATT_SCALE = HEAD_DIM ** -0.5
PAGE = 128

LANES = 128
SUBLANES = 8
ZF_COL = 5 * 512
W_IN_COLS = ZF_COL + LANES
META_PAD = 128
SMALL_ROWS = META_PAD + 32
VMEM_LIMIT = 60 * 1024 * 1024

TM = 512
TQ = 512
TK = 512
FF_CHUNKS = ((0, 1536), (1536, 1280))
PAGES_PER_CHUNK = 8
N_SLOTS = 4

bf16 = jnp.bfloat16
f32 = jnp.float32


def _rms(x, g):
    return x * lax.rsqrt(jnp.mean(x * x, axis=-1, keepdims=True) + EPS) * g


def _gelu(x):
    return x * (0.5 * (1.0 + jnp.tanh(0.7978845608028654 * (x + 0.044715 * (x * x * x)))))


def _sigmoid(x):
    return 1.0 / (1.0 + jnp.exp(-x))


def _softplus(x):
    return jnp.maximum(x, 0.0) + jnp.log1p(jnp.exp(-jnp.abs(x)))


def _log_sigmoid(x):
    return -_softplus(-x)


def _dot(a, b):
    return jnp.dot(a, b, preferred_element_type=f32)


def _shift_rows(x, s, fill):
    row = lax.broadcasted_iota(jnp.int32, x.shape, 0)
    return jnp.where(row >= s, pltpu.roll(x, s, 0), fill)


def _cumsum_lanes(x):
    n = x.shape[-1]
    lane = lax.broadcasted_iota(jnp.int32, x.shape, x.ndim - 1)
    s = 1
    while s < n:
        x = x + jnp.where(lane >= s, pltpu.roll(x, s, x.ndim - 1), 0.0)
        s *= 2
    return x


def _scan_rows(a, u):
    n = a.shape[0]
    s = 1
    while s < n:
        a_sh = _shift_rows(a, s, 1.0)
        u_sh = _shift_rows(u, s, 0.0)
        u = a * u_sh + u
        a = a * a_sh
        s *= 2
    return a, u


def _gates(xc, wg_ref, bga, bgx, lam):
    xcb = xc.astype(bf16)
    half = D_RNN // 2
    g0 = _dot(xcb[:, :half], wg_ref[0])
    g1 = _dot(xcb[:, half:], wg_ref[1])
    ga = jnp.concatenate([g0[:, :half], g1[:, :half]], axis=1) + bga
    gx = jnp.concatenate([g0[:, half:], g1[:, half:]], axis=1) + bgx
    r = _sigmoid(ga)
    ig = _sigmoid(gx)
    log_a = (-RG_C) * r * _softplus(-lam)
    a = jnp.exp(log_a)
    th = jnp.tanh(log_a)
    u = jnp.sqrt((-2.0 * th) / (1.0 - th)) * (ig * xc)
    return a, u


def _inproj_kernel(x_ref, nmix_ref, w_ref, bfc_ref, cw_ref, cb_ref, wg_ref, bga_ref, bgx_ref, lam_ref,
                   nrnn_ref, h0_ref, xr0_ref, c0_ref,
                   q_ref, kT_ref, vT_ref, kTb_ref, vb_ref, lfT_ref, cT_ref, rn_ref, hlast_ref, xrlast_ref,
                   xp_scr, a_scr, u_scr, h_scr, hc_scr, cc_scr):
    i = pl.program_id(1)

    @pl.when(i == 0)
    def _():
        xp_scr[0:SUBLANES, :] = xr0_ref[...]
        hc_scr[...] = h0_ref[SUBLANES - 1:SUBLANES, :]
        cc_scr[...] = c0_ref[:, N_META - 1:N_META]

    hn = _rms(x_ref[0], nmix_ref[...]).astype(bf16)
    z = _dot(hn, w_ref[...])
    q_ref[0] = z[:, 0:512].astype(bf16)
    kT = z[:, 512:1024].T
    kT_ref[0] = kT
    kTb_ref[0] = kT.astype(bf16)
    v = z[:, 1024:1536]
    vT_ref[0] = v.T
    vb_ref[0] = v.astype(bf16)
    xr = z[:, 1536:2048]
    gr = z[:, 2048:2560]

    lfT = _log_sigmoid(z[:, ZF_COL:ZF_COL + LANES].T[0:N_HEADS, :] + bfc_ref[...])
    lfT_ref[0] = lfT
    cum = _cumsum_lanes(lfT) + cc_scr[...]
    cT_ref[0] = cum
    cc_scr[...] = cum[:, TM - 1:TM]

    xp_scr[SUBLANES:SUBLANES + TM, :] = xr
    cw = cw_ref[...]
    xc = (cb_ref[...] + cw[0:1] * xp_scr[5:5 + TM, :] + cw[1:2] * xp_scr[6:6 + TM, :]
          + cw[2:3] * xp_scr[7:7 + TM, :] + cw[3:4] * xr)
    xp_scr[0:SUBLANES, :] = xr[TM - SUBLANES:TM, :]
    xrlast_ref[0] = xr[TM - SUBLANES:TM, :]

    a, u = _gates(xc, wg_ref, bga_ref[...], bgx_ref[...], lam_ref[...])
    a_scr[...] = a
    u_scr[...] = u

    def group(g, hprev):
        r0 = pl.multiple_of(g * SUBLANES, SUBLANES)
        ag, ug = _scan_rows(a_scr[pl.ds(r0, SUBLANES), :], u_scr[pl.ds(r0, SUBLANES), :])
        hg = ag * hprev + ug
        h_scr[pl.ds(r0, SUBLANES), :] = hg
        return hg[SUBLANES - 1:SUBLANES, :]

    hl = lax.fori_loop(0, TM // SUBLANES, group, hc_scr[...], unroll=4)
    hc_scr[...] = hl
    hlast_ref[0] = hl
    y = h_scr[...] * _gelu(gr)
    rn_ref[0] = _rms(y, nrnn_ref[...]).astype(bf16)


def _inproj_call(x_real, nmix, w_cat, bf_col, cw, cb, wg, bga, bgx, lam, nrnn, h0, xr0, c0):
    nb, t, _ = x_real.shape
    nt = t // TM
    const = lambda shape: pl.BlockSpec(shape, lambda b, i: (0,) * len(shape), pipeline_mode=pl.Buffered(1))
    row_blk = lambda w: pl.BlockSpec((1, TM, w), lambda b, i: (b, i, 0))
    col_blk = lambda r: pl.BlockSpec((1, r, TM), lambda b, i: (b, 0, i))
    per_b = lambda r, w: pl.BlockSpec((1, r, w), lambda b, i: (b, 0, 0))
    return pl.pallas_call(
        _inproj_kernel,
        grid=(nb, nt),
        in_specs=[row_blk(D_MODEL), const((1, D_MODEL)), const((D_MODEL, W_IN_COLS)), const((N_HEADS, 1)),
                  const((4, D_RNN)), const((1, D_RNN)), const((2, 256, 512)), const((1, D_RNN)),
                  const((1, D_RNN)), const((1, D_RNN)), const((1, D_RNN)),
                  const((SUBLANES, D_RNN)), const((SUBLANES, D_RNN)), const((N_HEADS, LANES))],
        out_specs=[row_blk(D_ATT), col_blk(D_ATT), col_blk(D_ATT), col_blk(D_ATT), row_blk(D_ATT),
                   col_blk(N_HEADS), col_blk(N_HEADS), row_blk(D_RNN), per_b(1, D_RNN), per_b(SUBLANES, D_RNN)],
        out_shape=[jax.ShapeDtypeStruct((nb, t, D_ATT), bf16),
                   jax.ShapeDtypeStruct((nb, D_ATT, t), f32),
                   jax.ShapeDtypeStruct((nb, D_ATT, t), f32),
                   jax.ShapeDtypeStruct((nb, D_ATT, t), bf16),
                   jax.ShapeDtypeStruct((nb, t, D_ATT), bf16),
                   jax.ShapeDtypeStruct((nb, N_HEADS, t), f32),
                   jax.ShapeDtypeStruct((nb, N_HEADS, t), f32),
                   jax.ShapeDtypeStruct((nb, t, D_RNN), bf16),
                   jax.ShapeDtypeStruct((nb, 1, D_RNN), f32),
                   jax.ShapeDtypeStruct((nb, SUBLANES, D_RNN), f32)],
        scratch_shapes=[pltpu.VMEM((TM + SUBLANES, D_RNN), f32), pltpu.VMEM((TM, D_RNN), f32),
                        pltpu.VMEM((TM, D_RNN), f32), pltpu.VMEM((TM, D_RNN), f32),
                        pltpu.VMEM((1, D_RNN), f32), pltpu.VMEM((N_HEADS, 1), f32)],
        compiler_params=pltpu.CompilerParams(dimension_semantics=("arbitrary", "arbitrary"),
                                             vmem_limit_bytes=VMEM_LIMIT),
        name="inproj_rglru",
    )(x_real, nmix, w_cat, bf_col, cw, cb, wg, bga, bgx, lam, nrnn, h0, xr0, c0)


def _softmax_step(carry, qj, kT, v, c, mask):
    m, l, acc = carry
    s = _dot(qj, kT) - c
    if mask is not None:
        s = jnp.where(mask, s, -jnp.inf)
    m_new = jnp.maximum(m, jnp.max(s, axis=1, keepdims=True))
    alpha = jnp.exp(m - m_new)
    p = jnp.exp(s - m_new)
    l = alpha * l + jnp.sum(p, axis=1, keepdims=True)
    acc = alpha * acc + _dot(p.astype(bf16), v)
    return m_new, l, acc


def _attn_kernel(q_ref, kT_ref, v_ref, c_ref, kTm_ref, vm_ref, cm_ref, o_ref):
    i = pl.program_id(2)
    q = q_ref[0]
    lane = lax.broadcasted_iota(jnp.int32, (TQ, LANES), 1)
    qs = [jnp.where(lane < HEAD_DIM, q, jnp.zeros_like(q)), jnp.where(lane >= HEAD_DIM, q, jnp.zeros_like(q))]

    meta_mask = lane < N_META
    carries = []
    for j in range(2):
        init = (jnp.full((TQ, 1), -jnp.inf, f32), jnp.zeros((TQ, 1), f32), jnp.zeros((TQ, LANES), f32))
        carries.append(_softmax_step(init, qs[j], kTm_ref[...], vm_ref[...], cm_ref[0, j:j + 1, :], meta_mask))

    def body(kt, cs):
        k0 = pl.multiple_of(kt * TK, TK)
        kT = kT_ref[0, :, pl.ds(k0, TK)]
        v = v_ref[0, pl.ds(k0, TK), :]
        return tuple(_softmax_step(cs[j], qs[j], kT, v, c_ref[0, 0, j:j + 1, pl.ds(k0, TK)], None)
                     for j in range(2))

    carries = lax.fori_loop(0, i * (TQ // TK), body, tuple(carries))

    row = lax.broadcasted_iota(jnp.int32, (TQ, TK), 0)
    col = lax.broadcasted_iota(jnp.int32, (TQ, TK), 1)
    for d in range(TQ // TK):
        k0 = pl.multiple_of(i * TQ + d * TK, TK)
        kT = kT_ref[0, :, pl.ds(k0, TK)]
        v = v_ref[0, pl.ds(k0, TK), :]
        mask = col + d * TK <= row
        carries = tuple(_softmax_step(carries[j], qs[j], kT, v, c_ref[0, 0, j:j + 1, pl.ds(k0, TK)], mask)
                        for j in range(2))

    outs = [acc / l for (_, l, acc) in carries]
    o_ref[0] = jnp.where(lane < HEAD_DIM, outs[0], outs[1])


def _attn_call(q, kTb, vb, c4, kTm, vm, cm4):
    nb, t, _ = q.shape
    npair = N_HEADS // 2
    return pl.pallas_call(
        _attn_kernel,
        grid=(nb, npair, t // TQ),
        in_specs=[pl.BlockSpec((1, TQ, LANES), lambda b, h, i: (b, i, h)),
                  pl.BlockSpec((1, LANES, t), lambda b, h, i: (b, h, 0)),
                  pl.BlockSpec((1, t, LANES), lambda b, h, i: (b, 0, h)),
                  pl.BlockSpec((1, 1, 2, t), lambda b, h, i: (b, h, 0, 0)),
                  pl.BlockSpec((LANES, META_PAD), lambda b, h, i: (h, 0)),
                  pl.BlockSpec((META_PAD, LANES), lambda b, h, i: (0, h)),
                  pl.BlockSpec((1, 2, META_PAD), lambda b, h, i: (h, 0, 0))],
        out_specs=pl.BlockSpec((1, TQ, LANES), lambda b, h, i: (b, i, h)),
        out_shape=jax.ShapeDtypeStruct((nb, t, D_ATT), f32),
        compiler_params=pltpu.CompilerParams(dimension_semantics=("arbitrary", "arbitrary", "arbitrary"),
                                             vmem_limit_bytes=VMEM_LIMIT),
        name="prompt_attention",
    )(q, kTb, vb, c4, kTm, vm, cm4)


def _ffn_kernel(x_ref, att_ref, rn_ref, natt_ref, woa_ref, wor_ref, nffn_ref, wup_ref, fw_ref, fb_ref,
                wdn_ref, nfin_ref, gp0_ref, y_ref, gplast_ref, gp_scr):
    i = pl.program_id(1)

    @pl.when(i == 0)
    def _():
        gp_scr[0:SUBLANES, :] = gp0_ref[...]

    an = _rms(att_ref[0], natt_ref[...]).astype(bf16)
    x1 = x_ref[0] + _dot(an, woa_ref[...]) + _dot(rn_ref[0], wor_ref[...])
    hn = _rms(x1, nffn_ref[...]).astype(bf16)
    down = None
    for c0, wd in FF_CHUNKS:
        cols = slice(c0, c0 + wd)
        gp = _dot(hn, wup_ref[:, cols])
        val = _dot(hn, wup_ref[:, D_FF + c0:D_FF + c0 + wd])
        gp_scr[SUBLANES:SUBLANES + TM, cols] = gp
        fw = fw_ref[:, cols]
        g = (fb_ref[:, cols] + fw[0:1] * gp_scr[6:6 + TM, cols] + fw[1:2] * gp_scr[7:7 + TM, cols]
             + fw[2:3] * gp)
        act = (_gelu(g) * val).astype(bf16)
        part = _dot(act, wdn_ref[cols, :])
        down = part if down is None else down + part
    last = gp_scr[TM:TM + SUBLANES, :]
    gp_scr[0:SUBLANES, :] = last
    gplast_ref[0] = last
    y_ref[0] = _rms(x1 + down, nfin_ref[...])


def _ffn_call(x_real, att, rn, natt, woa, wor, nffn, wup, fw, fb, wdn, nfin, gp0):
    nb, t, _ = x_real.shape
    const = lambda shape: pl.BlockSpec(shape, lambda b, i: (0,) * len(shape), pipeline_mode=pl.Buffered(1))
    row_blk = lambda w: pl.BlockSpec((1, TM, w), lambda b, i: (b, i, 0))
    return pl.pallas_call(
        _ffn_kernel,
        grid=(nb, t // TM),
        in_specs=[row_blk(D_MODEL), row_blk(D_ATT), row_blk(D_RNN), const((1, D_ATT)),
                  const((D_ATT, D_MODEL)), const((D_RNN, D_MODEL)), const((1, D_MODEL)),
                  const((D_MODEL, 2 * D_FF)), const((3, D_FF)), const((1, D_FF)),
                  const((D_FF, D_MODEL)), const((1, D_MODEL)), const((SUBLANES, D_FF))],
        out_specs=[row_blk(D_MODEL), pl.BlockSpec((1, SUBLANES, D_FF), lambda b, i: (b, 0, 0))],
        out_shape=[jax.ShapeDtypeStruct((nb, t, D_MODEL), f32),
                   jax.ShapeDtypeStruct((nb, SUBLANES, D_FF), f32)],
        scratch_shapes=[pltpu.VMEM((TM + SUBLANES, D_FF), f32)],
        compiler_params=pltpu.CompilerParams(dimension_semantics=("arbitrary", "arbitrary"),
                                             vmem_limit_bytes=VMEM_LIMIT),
        name="outproj_ffn",
    )(x_real, att, rn, natt, woa, wor, nffn, wup, fw, fb, wdn, nfin, gp0)


def _small_pre_kernel(x_ref, nmix_ref, w_ref, bfc_ref, bfr_ref, cw_ref, cb_ref, wg_ref, bga_ref, bgx_ref,
                      lam_ref, nrnn_ref, hs0_ref, cs0_ref, cs1_ref, cs2_ref,
                      qs_ref, ks_ref, vs_ref, lfs_ref, hs_ref, xrs_ref,
                      kTm_ref, vTm_ref, km_ref, vm_ref, lfTm_ref, cTm_ref, attm_ref, hm_ref, xrm_ref, rn_ref):
    m0 = META_PAD
    hn = _rms(x_ref[...], nmix_ref[...]).astype(bf16)
    z = _dot(hn, w_ref[...])
    cw = cw_ref[...]
    lam = lam_ref[...]

    zs = z[m0:, :]
    qs_ref[...] = zs[:, 0:512]
    ks_ref[...] = zs[:, 512:1024]
    vs_ref[...] = zs[:, 1024:1536]
    lfs_ref[...] = _log_sigmoid(zs[:, ZF_COL:ZF_COL + N_HEADS] + bfr_ref[...])
    xrs = zs[:, 1536:2048]
    xrs_ref[...] = xrs
    xcs = cb_ref[...] + cw[0:1] * cs0_ref[...] + cw[1:2] * cs1_ref[...] + cw[2:3] * cs2_ref[...] + cw[3:4] * xrs
    a_s, u_s = _gates(xcs, wg_ref, bga_ref[...], bgx_ref[...], lam)
    h_s = a_s * hs0_ref[...] + u_s
    hs_ref[...] = h_s
    rn_ref[m0:, :] = _rms(h_s * _gelu(zs[:, 2048:2560]), nrnn_ref[...])

    zm = z[0:m0, :]
    km = zm[:, 512:1024]
    vm = zm[:, 1024:1536]
    kTm = km.T
    kTm_ref[...] = kTm
    vTm_ref[...] = vm.T
    km_ref[...] = km
    vm_ref[...] = vm
    lfT = _log_sigmoid(zm[:, ZF_COL:ZF_COL + LANES].T[0:N_HEADS, :] + bfc_ref[...])
    lfTm_ref[...] = lfT
    cT = _cumsum_lanes(lfT)
    cTm_ref[...] = cT

    xrm = zm[:, 1536:2048]
    xrm_ref[...] = xrm[SUBLANES:2 * SUBLANES, :]
    xcm = (cb_ref[...] + cw[0:1] * _shift_rows(xrm, 3, 0.0) + cw[1:2] * _shift_rows(xrm, 2, 0.0)
           + cw[2:3] * _shift_rows(xrm, 1, 0.0) + cw[3:4] * xrm)
    a_m, u_m = _gates(xcm[0:N_META], wg_ref, bga_ref[...], bgx_ref[...], lam)
    _, h_m = _scan_rows(a_m, u_m)
    hm_ref[...] = h_m[SUBLANES:2 * SUBLANES, :]
    rn_ref[0:N_META, :] = _rms(h_m * _gelu(zm[0:N_META, 2048:2560]), nrnn_ref[...])
    rn_ref[N_META:m0, :] = jnp.zeros((m0 - N_META, D_RNN), f32)

    qm = zm[:, 0:512].astype(bf16)
    kTb = kTm.astype(bf16)
    vb = vm.astype(bf16)
    lane = lax.broadcasted_iota(jnp.int32, (m0, LANES), 1)
    row = lax.broadcasted_iota(jnp.int32, (m0, LANES), 0)
    mask = (lane <= row) & (lane < N_META)
    for hp in range(N_HEADS // 2):
        sl = slice(hp * LANES, (hp + 1) * LANES)
        qp = qm[:, sl]
        outs = []
        for j in range(2):
            h = 2 * hp + j
            qj = jnp.where((lane >= j * HEAD_DIM) & (lane < (j + 1) * HEAD_DIM), qp, jnp.zeros_like(qp))
            s = _dot(qj, kTb[sl, :]) - cT[h:h + 1, :]
            s = jnp.where(mask, s, -jnp.inf)
            p = jnp.exp(s - jnp.max(s, axis=1, keepdims=True))
            o = _dot(p.astype(bf16), vb[:, sl]) / jnp.sum(p, axis=1, keepdims=True)
            outs.append(o)
        attm_ref[:, sl] = jnp.where(lane < HEAD_DIM, outs[0], outs[1])


def _small_pre_call(xs, nmix, w_cat, bf_col, bf_row, cw, cb, wg, bga, bgx, lam, nrnn, hs0, cs0, cs1, cs2):
    ns = SMALL_ROWS - META_PAD
    shp = jax.ShapeDtypeStruct
    return pl.pallas_call(
        _small_pre_kernel,
        out_shape=[shp((ns, D_ATT), f32), shp((ns, D_ATT), f32), shp((ns, D_ATT), f32), shp((ns, N_HEADS), f32),
                   shp((ns, D_RNN), f32), shp((ns, D_RNN), f32),
                   shp((D_ATT, META_PAD), f32), shp((D_ATT, META_PAD), f32),
                   shp((META_PAD, D_ATT), f32), shp((META_PAD, D_ATT), f32),
                   shp((N_HEADS, META_PAD), f32), shp((N_HEADS, META_PAD), f32),
                   shp((META_PAD, D_ATT), f32), shp((SUBLANES, D_RNN), f32), shp((SUBLANES, D_RNN), f32),
                   shp((SMALL_ROWS, D_RNN), f32)],
        compiler_params=pltpu.CompilerParams(vmem_limit_bytes=VMEM_LIMIT),
        name="small_pre",
    )(xs, nmix, w_cat, bf_col, bf_row, cw, cb, wg, bga, bgx, lam, nrnn, hs0, cs0, cs1, cs2)


def _small_post_kernel(x_ref, att_ref, rn_ref, natt_ref, woa_ref, wor_ref, nffn_ref, wup_ref, fw_ref, fb_ref,
                       wdn_ref, nfin_ref, f0_ref, f1_ref, ys_ref, gps_ref, gpm_ref):
    m0 = META_PAD
    an = _rms(att_ref[...], natt_ref[...]).astype(bf16)
    x1 = x_ref[...] + _dot(an, woa_ref[...]) + _dot(rn_ref[...].astype(bf16), wor_ref[...])
    hn = _rms(x1, nffn_ref[...]).astype(bf16)
    up = _dot(hn, wup_ref[...])
    gp = up[:, 0:D_FF]
    val = up[:, D_FF:]
    fw = fw_ref[...]
    gpm = gp[0:m0]
    gm = fb_ref[...] + fw[0:1] * _shift_rows(gpm, 2, 0.0) + fw[1:2] * _shift_rows(gpm, 1, 0.0) + fw[2:3] * gpm
    gps = gp[m0:]
    gs = fb_ref[...] + fw[0:1] * f0_ref[...] + fw[1:2] * f1_ref[...] + fw[2:3] * gps
    gps_ref[...] = gps
    gpm_ref[...] = gpm[SUBLANES:2 * SUBLANES, :]
    g = jnp.concatenate([gm, gs], axis=0)
    act = (_gelu(g) * val).astype(bf16)
    x2 = x1 + _dot(act, wdn_ref[...])
    ys_ref[...] = _rms(x2[m0:], nfin_ref[...])


def _small_post_call(xs, att, rn, natt, woa, wor, nffn, wup, fw, fb, wdn, nfin, f0, f1):
    ns = SMALL_ROWS - META_PAD
    shp = jax.ShapeDtypeStruct
    return pl.pallas_call(
        _small_post_kernel,
        out_shape=[shp((ns, D_MODEL), f32), shp((ns, D_FF), f32), shp((SUBLANES, D_FF), f32)],
        compiler_params=pltpu.CompilerParams(vmem_limit_bytes=VMEM_LIMIT),
        name="small_post",
    )(xs, att, rn, natt, woa, wor, nffn, wup, fw, fb, wdn, nfin, f0, f1)


def _paged_kernel(pt_ref, qbc_ref, q_ref, kn_ref, vn_ref, lfn_ref, ck_ref, cv_ref, cl_ref, o_ref,
                  buf, lbuf, s_scr, acc_scr, sem, lsem, *, n_seq, n_pages):
    b = pl.program_id(0)
    n_chunks = n_pages // PAGES_PER_CHUNK
    per_seq = 2 * n_chunks

    def chunk_copies(seq, c, slot):
        src = ck_ref if c < n_chunks else cv_ref
        p0 = (c % n_chunks) * PAGES_PER_CHUNK
        return [pltpu.make_async_copy(src.at[pt_ref[seq, p0 + pg]], buf.at[slot, pg], sem.at[slot])
                for pg in range(PAGES_PER_CHUNK)]

    def logf_copies(seq, slot):
        return [pltpu.make_async_copy(cl_ref.at[pt_ref[seq, pg]], lbuf.at[slot, pg], lsem.at[slot])
                for pg in range(n_pages)]

    lslot = b % 2
    nxt = jnp.minimum(b + 1, n_seq - 1)

    @pl.when(b == 0)
    def _():
        for cp in logf_copies(0, 0):
            cp.start()
        for c in range(N_SLOTS - 1):
            for cp in chunk_copies(0, c, c):
                cp.start()

    @pl.when(b + 1 < n_seq)
    def _():
        for cp in logf_copies(nxt, 1 - lslot):
            cp.start()

    acc_scr[...] = jnp.zeros(acc_scr.shape, f32)

    for c in range(per_seq):
        ahead = c + N_SLOTS - 1
        if ahead < per_seq:
            for cp in chunk_copies(b, ahead, ahead % N_SLOTS):
                cp.start()
        else:
            @pl.when(b + 1 < n_seq)
            def _():
                for cp in chunk_copies(nxt, ahead - per_seq, ahead % N_SLOTS):
                    cp.start()
        slot = c % N_SLOTS
        for cp in chunk_copies(b, c, slot):
            cp.wait()

        if c < n_chunks:
            for h in range(N_HEADS):
                qh = qbc_ref[0, h]
                for pg in range(PAGES_PER_CHUNK):
                    red = jnp.sum(buf[slot, pg, h] * qh, axis=0, keepdims=True)
                    s_scr[c * PAGES_PER_CHUNK + pg, h:h + 1, :] = red
        else:
            for h in range(N_HEADS):
                a = jnp.zeros((HEAD_DIM, PAGE), f32)
                for pg in range(PAGES_PER_CHUNK):
                    page = (c - n_chunks) * PAGES_PER_CHUNK + pg
                    a = a + buf[slot, pg, h] * s_scr[page, h:h + 1, :]
                acc_scr[h] += a

        if c == n_chunks - 1:
            for cp in logf_copies(b, lslot):
                cp.wait()
            rows = n_pages * N_HEADS
            lf = lbuf[lslot].reshape(rows, PAGE)
            cw = _cumsum_lanes(lf)
            tot = jnp.broadcast_to(cw[:, PAGE - 1:PAGE], (rows, PAGE))
            ridx = lax.broadcasted_iota(jnp.int32, (rows, PAGE), 0)
            off = tot
            sh = N_HEADS
            while sh < rows:
                off = off + jnp.where(ridx >= sh, pltpu.roll(off, sh, 0), 0.0)
                sh *= 2
            cfull = cw + (off - tot)
            s3 = (s_scr[...].reshape(rows, PAGE) - cfull).reshape(n_pages, N_HEADS, PAGE)
            c_past = off[rows - N_HEADS:rows, 0:1]
            s_new = jnp.sum(q_ref[0] * kn_ref[0], axis=1, keepdims=True) - (c_past + lfn_ref[0])
            m = jnp.max(jnp.max(s3, axis=0), axis=1, keepdims=True)
            m = jnp.maximum(m, s_new)
            p3 = jnp.exp(s3 - m[None])
            p_new = jnp.exp(s_new - m)
            l = jnp.sum(jnp.sum(p3, axis=0), axis=1, keepdims=True) + p_new
            s_scr[...] = p3

    num = jnp.sum(acc_scr[...], axis=2) + p_new * vn_ref[0]
    o_ref[0] = num / l


def _paged_call(page_table, qbc, q3, kn3, vn3, lfn3, ckT, cvT, clT):
    n_seq, n_pages = page_table.shape
    blk3 = lambda w: pl.BlockSpec((1, N_HEADS, w), lambda b, pt: (b, 0, 0))
    grid_spec = pltpu.PrefetchScalarGridSpec(
        num_scalar_prefetch=1,
        grid=(n_seq,),
        in_specs=[pl.BlockSpec((1, N_HEADS, HEAD_DIM, PAGE), lambda b, pt: (b, 0, 0, 0)),
                  blk3(HEAD_DIM), blk3(HEAD_DIM), blk3(HEAD_DIM), blk3(1),
                  pl.BlockSpec(memory_space=pl.ANY), pl.BlockSpec(memory_space=pl.ANY),
                  pl.BlockSpec(memory_space=pl.ANY)],
        out_specs=blk3(HEAD_DIM),
        scratch_shapes=[pltpu.VMEM((N_SLOTS, PAGES_PER_CHUNK, N_HEADS, HEAD_DIM, PAGE), f32),
                        pltpu.VMEM((2, n_pages, N_HEADS, PAGE), f32),
                        pltpu.VMEM((n_pages, N_HEADS, PAGE), f32),
                        pltpu.VMEM((N_HEADS, HEAD_DIM, PAGE), f32),
                        pltpu.SemaphoreType.DMA((N_SLOTS,)), pltpu.SemaphoreType.DMA((2,))])
    return pl.pallas_call(
        functools.partial(_paged_kernel, n_seq=n_seq, n_pages=n_pages),
        grid_spec=grid_spec,
        out_shape=jax.ShapeDtypeStruct((n_seq, N_HEADS, HEAD_DIM), f32),
        compiler_params=pltpu.CompilerParams(dimension_semantics=("arbitrary",), vmem_limit_bytes=VMEM_LIMIT),
        name="paged_attention",
    )(page_table, qbc, q3, kn3, vn3, lfn3, ckT, cvT, clT)


def _block_diag(w):
    n, r, c = w.shape
    eye = jnp.eye(n, dtype=w.dtype)
    return (eye[:, None, :, None] * w[:, :, None, :]).reshape(n * r, n * c)


def kernel(x_prompt, x_sample, cache_k, cache_v, cache_logf, state_rnn_h, state_rnn_conv, state_ffn_conv, page_table, meta_tokens, norm_mix, w_in, b_forget, rnn_conv_w, rnn_conv_b, w_rg_a, b_rg_a, w_rg_x, b_rg_x, rg_lambda, norm_att_out, norm_rnn_out, w_out, norm_ffn, w_ffn_up, ffn_conv_w, ffn_conv_b, w_ffn_down, norm_final):
    nb, seq, _ = x_prompt.shape
    db = x_sample.shape[0]
    assert norm_mix.shape[0] == 1 and x_sample.shape[1] == 1 and seq % TM == 0 and seq % TQ == 0

    w = w_in[0]
    wf = jnp.pad(w[:, 3 * D_ATT:3 * D_ATT + N_HEADS], ((0, 0), (0, LANES - N_HEADS)))
    w_cat = jnp.concatenate([w[:, :D_ATT] * ATT_SCALE, w[:, D_ATT:3 * D_ATT], w[:, 3 * D_ATT + N_HEADS:], wf],
                            axis=1).astype(bf16)
    bda = _block_diag(w_rg_a[0])
    bdx = _block_diag(w_rg_x[0])
    wg = jnp.stack([jnp.concatenate([bda[:256, :256], bdx[:256, :256]], axis=1),
                    jnp.concatenate([bda[256:, 256:], bdx[256:, 256:]], axis=1)]).astype(bf16)
    woa = w_out[0, :D_ATT].astype(bf16)
    wor = w_out[0, D_ATT:].astype(bf16)
    wup = w_ffn_up[0].astype(bf16)
    wdn = w_ffn_down[0].astype(bf16)
    bf_col = b_forget[0][:, None]
    bf_row = b_forget
    nfin = norm_final[None, :]
    mix_args = (norm_mix, w_cat, bf_col)
    rnn_args = (rnn_conv_w[0], rnn_conv_b, wg, b_rg_a, b_rg_x, rg_lambda, norm_rnn_out)
    post_args = (norm_att_out, woa, wor, norm_ffn, wup, ffn_conv_w[0], ffn_conv_b, wdn, nfin)

    xs = jnp.concatenate([meta_tokens, jnp.zeros((META_PAD - N_META, D_MODEL), f32), x_sample[:, 0, :]], axis=0)
    (q_s, k_s, v_s, lf_s, h_s, xr_s, kTm, vTm, km, vm, lfTm, cTm, att_m, h_m, xr_m, rn_small) = _small_pre_call(
        xs, *mix_args, bf_row, *rnn_args, state_rnn_h[0],
        state_rnn_conv[0, :, 0], state_rnn_conv[0, :, 1], state_rnn_conv[0, :, 2])

    ckT = jnp.transpose(cache_k[0], (0, 2, 3, 1))
    cvT = jnp.transpose(cache_v[0], (0, 2, 3, 1))
    clT = jnp.transpose(cache_logf[0], (0, 2, 1))
    q3 = q_s.reshape(db, N_HEADS, HEAD_DIM)
    qbc = jnp.broadcast_to(q3[:, :, :, None], (db, N_HEADS, HEAD_DIM, PAGE))
    att_s = _paged_call(page_table, qbc, q3, k_s.reshape(db, N_HEADS, HEAD_DIM),
                        v_s.reshape(db, N_HEADS, HEAD_DIM), lf_s[:, :, None], ckT, cvT, clT)

    att_small = jnp.concatenate([att_m, att_s.reshape(db, D_ATT)], axis=0)
    y_s, gp_s, gp_m = _small_post_call(xs, att_small, rn_small, *post_args,
                                       state_ffn_conv[0, :, 0], state_ffn_conv[0, :, 1])

    (q, kT, vT, kTb, vb, lfT, cT, rn, h_last, xr_last) = _inproj_call(
        x_prompt, *mix_args, *rnn_args, h_m, xr_m, cTm)
    npair = N_HEADS // 2
    att = _attn_call(q, kTb, vb, cT.reshape(nb, npair, 2, seq), kTm.astype(bf16), vm.astype(bf16),
                     cTm.reshape(npair, 2, META_PAD))
    y_prompt, gp_last = _ffn_call(x_prompt, att, rn, *post_args, gp_m)

    def with_meta(meta_t, real_t):
        c = meta_t.shape[0]
        return jnp.concatenate([jnp.broadcast_to(meta_t[None, :, :N_META], (nb, c, N_META)), real_t], axis=2)

    t_all = N_META + seq
    k_prompt = jnp.transpose(with_meta(kTm, kT).reshape(nb, N_HEADS, HEAD_DIM, t_all), (0, 3, 1, 2))[None]
    v_prompt = jnp.transpose(with_meta(vTm, vT).reshape(nb, N_HEADS, HEAD_DIM, t_all), (0, 3, 1, 2))[None]
    logf_prompt = jnp.transpose(with_meta(lfTm, lfT), (0, 2, 1))[None]
    rnn_h_prompt = h_last[:, 0, :][None]
    rnn_conv_prompt = xr_last[:, SUBLANES - 3:, :][None]
    ffn_conv_prompt = gp_last[:, SUBLANES - 2:, :][None]

    k_sample = k_s.reshape(1, db, 1, N_HEADS, HEAD_DIM)
    v_sample = v_s.reshape(1, db, 1, N_HEADS, HEAD_DIM)
    logf_sample = lf_s.reshape(1, db, 1, N_HEADS)
    rnn_h_sample = h_s[None]
    rnn_conv_sample = jnp.stack([state_rnn_conv[0, :, 1], state_rnn_conv[0, :, 2], xr_s], axis=1)[None]
    ffn_conv_sample = jnp.stack([state_ffn_conv[0, :, 1], gp_s], axis=1)[None]
    y_sample = y_s[:, None, :]

    return (y_prompt, y_sample, k_prompt, v_prompt, logf_prompt, rnn_h_prompt, rnn_conv_prompt, ffn_conv_prompt,
            k_sample, v_sample, logf_sample, rnn_h_sample, rnn_conv_sample, ffn_conv_sample)
```

```python
import functools

import jax
import jax.numpy as jnp
from jax import lax
from jax.experimental import pallas as pl
from jax.experimental.pallas import tpu as pltpu

D_MODEL = 1024
N_META = 16
D_ATT = 512
D_RNN = 512
HEAD_DIM = 64
N_HEADS = 8
N_RNN_BLOCKS = 8
RG_C = 8.0
D_FF = 2816
EPS = 1e-6
ATT_SCALE = HEAD_DIM ** -0.5
PAGE = 128

LANES = 128
SUBLANES = 8
ZF_COL = 5 * 512
W_IN_COLS = ZF_COL + LANES
META_PAD = 128
SMALL_ROWS = META_PAD + 32
VMEM_LIMIT = 60 * 1024 * 1024

TM = 512
TQ = 512
TK = 512
ROW_STRIP = 32
BIAS_ROWS = 32
LOG2E = 1.4426950408889634
FF_CHUNKS = ((0, 1536), (1536, 1280))
PAGES_PER_CHUNK = 8
N_SLOTS = 4

bf16 = jnp.bfloat16
f32 = jnp.float32


def _rms(x, g):
    return x * lax.rsqrt(jnp.mean(x * x, axis=-1, keepdims=True) + EPS) * g


def _gelu(x):
    return x * (0.5 * (1.0 + jnp.tanh(0.7978845608028654 * (x + 0.044715 * (x * x * x)))))


def _sigmoid(x):
    return 1.0 / (1.0 + jnp.exp(-x))


def _softplus(x):
    return jnp.maximum(x, 0.0) + jnp.log1p(jnp.exp(-jnp.abs(x)))


def _log_sigmoid(x):
    return -_softplus(-x)


def _dot(a, b):
    return jnp.dot(a, b, preferred_element_type=f32)


def _shift_rows(x, s, fill):
    row = lax.broadcasted_iota(jnp.int32, x.shape, 0)
    return jnp.where(row >= s, pltpu.roll(x, s, 0), fill)


def _cumsum_lanes(x):
    n = x.shape[-1]
    lane = lax.broadcasted_iota(jnp.int32, x.shape, x.ndim - 1)
    s = 1
    while s < n:
        x = x + jnp.where(lane >= s, pltpu.roll(x, s, x.ndim - 1), 0.0)
        s *= 2
    return x


def _scan_rows(a, u):
    n = a.shape[0]
    s = 1
    while s < n:
        a_sh = _shift_rows(a, s, 1.0)
        u_sh = _shift_rows(u, s, 0.0)
        u = a * u_sh + u
        a = a * a_sh
        s *= 2
    return a, u


def _gates(xc, wg_ref, bga, bgx, lam):
    xcb = xc.astype(bf16)
    half = D_RNN // 2
    g0 = _dot(xcb[:, :half], wg_ref[0])
    g1 = _dot(xcb[:, half:], wg_ref[1])
    ga = jnp.concatenate([g0[:, :half], g1[:, :half]], axis=1) + bga
    gx = jnp.concatenate([g0[:, half:], g1[:, half:]], axis=1) + bgx
    r = _sigmoid(ga)
    ig = _sigmoid(gx)
    log_a = (-RG_C) * r * _softplus(-lam)
    a = jnp.exp(log_a)
    th = jnp.tanh(log_a)
    u = jnp.sqrt((-2.0 * th) / (1.0 - th)) * (ig * xc)
    return a, u


def _inproj_kernel(x_ref, nmix_ref, w_ref, bfc_ref, cw_ref, cb_ref, wg_ref, bga_ref, bgx_ref, lam_ref,
                   nrnn_ref, h0_ref, xr0_ref, c0_ref,
                   q_ref, kT_ref, vT_ref, kTb_ref, vb_ref, lfT_ref, ca_ref, rn_ref, hlast_ref, xrlast_ref,
                   xp_scr, a_scr, u_scr, h_scr, hc_scr, cc_scr):
    i = pl.program_id(1)

    @pl.when(i == 0)
    def _():
        xp_scr[0:SUBLANES, :] = xr0_ref[...]
        hc_scr[...] = h0_ref[SUBLANES - 1:SUBLANES, :]
        cc_scr[...] = c0_ref[:, N_META - 1:N_META]

    hn = _rms(x_ref[0], nmix_ref[...]).astype(bf16)
    z = _dot(hn, w_ref[...])
    q_ref[0] = (z[:, 0:512] * LOG2E).astype(bf16)
    kT = z[:, 512:1024].T
    kT_ref[0] = kT
    kTb_ref[0] = kT.astype(bf16)
    v = z[:, 1024:1536]
    vT_ref[0] = v.T
    vb_ref[0] = v.astype(bf16)
    xr = z[:, 1536:2048]
    gr = z[:, 2048:2560]

    lfT = _log_sigmoid(z[:, ZF_COL:ZF_COL + LANES].T[0:N_HEADS, :] + bfc_ref[...])
    lfT_ref[0] = lfT
    cum = _cumsum_lanes(lfT) + cc_scr[...]
    ca_ref[0] = _bias_parts(cum)
    cc_scr[...] = cum[:, TM - 1:TM]

    xp_scr[SUBLANES:SUBLANES + TM, :] = xr
    cw = cw_ref[...]
    xc = (cb_ref[...] + cw[0:1] * xp_scr[5:5 + TM, :] + cw[1:2] * xp_scr[6:6 + TM, :]
          + cw[2:3] * xp_scr[7:7 + TM, :] + cw[3:4] * xr)
    xp_scr[0:SUBLANES, :] = xr[TM - SUBLANES:TM, :]
    xrlast_ref[0] = xr[TM - SUBLANES:TM, :]

    a, u = _gates(xc, wg_ref, bga_ref[...], bgx_ref[...], lam_ref[...])
    a_scr[...] = a
    u_scr[...] = u

    def group(g, hprev):
        r0 = pl.multiple_of(g * SUBLANES, SUBLANES)
        ag, ug = _scan_rows(a_scr[pl.ds(r0, SUBLANES), :], u_scr[pl.ds(r0, SUBLANES), :])
        hg = ag * hprev + ug
        h_scr[pl.ds(r0, SUBLANES), :] = hg
        return hg[SUBLANES - 1:SUBLANES, :]

    hl = lax.fori_loop(0, TM // SUBLANES, group, hc_scr[...], unroll=4)
    hc_scr[...] = hl
    hlast_ref[0] = hl
    y = h_scr[...] * _gelu(gr)
    rn_ref[0] = _rms(y, nrnn_ref[...]).astype(bf16)


def _inproj_call(x_real, nmix, w_cat, bf_col, cw, cb, wg, bga, bgx, lam, nrnn, h0, xr0, c0):
    nb, t, _ = x_real.shape
    nt = t // TM
    const = lambda shape: pl.BlockSpec(shape, lambda b, i: (0,) * len(shape), pipeline_mode=pl.Buffered(1))
    row_blk = lambda w: pl.BlockSpec((1, TM, w), lambda b, i: (b, i, 0))
    col_blk = lambda r: pl.BlockSpec((1, r, TM), lambda b, i: (b, 0, i))
    per_b = lambda r, w: pl.BlockSpec((1, r, w), lambda b, i: (b, 0, 0))
    return pl.pallas_call(
        _inproj_kernel,
        grid=(nb, nt),
        in_specs=[row_blk(D_MODEL), const((1, D_MODEL)), const((D_MODEL, W_IN_COLS)), const((N_HEADS, 1)),
                  const((4, D_RNN)), const((1, D_RNN)), const((2, 256, 512)), const((1, D_RNN)),
                  const((1, D_RNN)), const((1, D_RNN)), const((1, D_RNN)),
                  const((SUBLANES, D_RNN)), const((SUBLANES, D_RNN)), const((N_HEADS, LANES))],
        out_specs=[row_blk(D_ATT), col_blk(D_ATT), col_blk(D_ATT), col_blk(D_ATT), row_blk(D_ATT),
                   col_blk(N_HEADS), col_blk(BIAS_ROWS), row_blk(D_RNN), per_b(1, D_RNN), per_b(SUBLANES, D_RNN)],
        out_shape=[jax.ShapeDtypeStruct((nb, t, D_ATT), bf16),
                   jax.ShapeDtypeStruct((nb, D_ATT, t), f32),
                   jax.ShapeDtypeStruct((nb, D_ATT, t), f32),
                   jax.ShapeDtypeStruct((nb, D_ATT, t), bf16),
                   jax.ShapeDtypeStruct((nb, t, D_ATT), bf16),
                   jax.ShapeDtypeStruct((nb, N_HEADS, t), f32),
                   jax.ShapeDtypeStruct((nb, BIAS_ROWS, t), bf16),
                   jax.ShapeDtypeStruct((nb, t, D_RNN), bf16),
                   jax.ShapeDtypeStruct((nb, 1, D_RNN), f32),
                   jax.ShapeDtypeStruct((nb, SUBLANES, D_RNN), f32)],
        scratch_shapes=[pltpu.VMEM((TM + SUBLANES, D_RNN), f32), pltpu.VMEM((TM, D_RNN), f32),
                        pltpu.VMEM((TM, D_RNN), f32), pltpu.VMEM((TM, D_RNN), f32),
                        pltpu.VMEM((1, D_RNN), f32), pltpu.VMEM((N_HEADS, 1), f32)],
        compiler_params=pltpu.CompilerParams(dimension_semantics=("arbitrary", "arbitrary"),
                                             vmem_limit_bytes=VMEM_LIMIT),
        name="inproj_rglru",
    )(x_real, nmix, w_cat, bf_col, cw, cb, wg, bga, bgx, lam, nrnn, h0, xr0, c0)


def _bias_parts(c):
    c2 = c * (-LOG2E)
    hi = c2.astype(bf16).astype(f32)
    r1 = c2 - hi
    mid = r1.astype(bf16).astype(f32)
    lo = (r1 - mid).astype(bf16).astype(f32)
    return jnp.concatenate([hi, mid, lo, jnp.zeros_like(hi)], axis=0).astype(bf16)


def _unit_softmax(j, c0, width, visible, s_scr, p_scr, m_scr, l_scr, acc_scr):
    for r0 in range(0, TQ, ROW_STRIP):
        rows = slice(r0, r0 + ROW_STRIP)
        st = []
        for t in range(width // LANES):
            vis = visible(r0, t)
            if vis is False:
                st.append(None)
                continue
            x = s_scr[j, rows, c0 + t * LANES:c0 + (t + 1) * LANES]
            st.append(x if vis is True else jnp.where(vis, x, -jnp.inf))
        live = [x for x in st if x is not None]
        mx = live[0]
        for x in live[1:]:
            mx = jnp.maximum(mx, x)
        m_old = m_scr[j, rows, :]
        m_new = jnp.maximum(m_old, jnp.max(mx, axis=1, keepdims=True))
        alpha = jnp.exp2(m_old - m_new)
        psum = None
        for t, x in enumerate(st):
            cols = slice(c0 + t * LANES, c0 + (t + 1) * LANES)
            if x is None:
                p_scr[j, rows, cols] = jnp.zeros((ROW_STRIP, LANES), bf16)
                continue
            p = jnp.exp2(x - m_new)
            psum = p if psum is None else psum + p
            p_scr[j, rows, cols] = p.astype(bf16)
        m_scr[j, rows, :] = m_new
        l_scr[j, rows, :] = alpha * l_scr[j, rows, :] + psum
        acc_scr[j, rows, :] = alpha * acc_scr[j, rows, :]


def _attn_units(units, qas, s_scr, p_scr, m_scr, l_scr, acc_scr):
    def scores(u):
        j, c0, width, kTa, _, _ = u
        s_scr[j, :, c0:c0 + width] = _dot(qas[j], kTa)

    def values(u):
        j, c0, width, _, v, _ = u
        acc_scr[j] += _dot(p_scr[j, :, c0:c0 + width], v)

    for u in units[:2]:
        scores(u)
    for n, u in enumerate(units):
        j, c0, width, _, _, visible = u
        _unit_softmax(j, c0, width, visible, s_scr, p_scr, m_scr, l_scr, acc_scr)
        if n + 2 < len(units):
            scores(units[n + 2])
        values(u)


def _attn_kernel(q_ref, kT_ref, v_ref, ca_ref, kTm_ref, vm_ref, cam_ref, o_ref, s_scr, p_scr, m_scr, l_scr, acc_scr):
    hp = pl.program_id(1)
    i = pl.program_id(2)
    q = q_ref[0].astype(f32)
    lane = lax.broadcasted_iota(jnp.int32, (TQ, LANES), 1)
    qas = []
    for j in range(2):
        h = 2 * hp + j
        qj = jnp.where((lane >= j * HEAD_DIM) & (lane < (j + 1) * HEAD_DIM), q, 0.0)
        pick = jnp.where((lane == h) | (lane == h + N_HEADS) | (lane == h + 2 * N_HEADS), 1.0, 0.0)
        qas.append(jnp.concatenate([qj, pick], axis=1).astype(bf16))
    scr = (s_scr, p_scr, m_scr, l_scr, acc_scr)
    m_scr[...] = jnp.full(m_scr.shape, -jnp.inf, f32)
    l_scr[...] = jnp.zeros(l_scr.shape, f32)
    acc_scr[...] = jnp.zeros(acc_scr.shape, f32)

    def keys_aug(k0):
        pad = jnp.zeros((LANES - BIAS_ROWS, TK), bf16)
        return jnp.concatenate([kT_ref[0, :, pl.ds(k0, TK)], ca_ref[0, :, pl.ds(k0, TK)], pad], axis=0)

    def full_tiles(k0, n_tiles):
        units = []
        for a in range(n_tiles):
            ka = pl.multiple_of(k0 + a * TK, TK)
            kTa = keys_aug(ka)
            v = v_ref[0, pl.ds(ka, TK), :]
            units += [(j, a * TK, TK, kTa, v, lambda r0, t: True) for j in range(2)]
        _attn_units(units, qas, *scr)

    def body(kt, carry):
        full_tiles(kt * (2 * TK), 2)
        return carry

    lax.fori_loop(0, i // 2, body, 0)

    @pl.when(i % 2 == 1)
    def _():
        full_tiles((i - 1) * TK, 1)

    slane = lax.broadcasted_iota(jnp.int32, (ROW_STRIP, LANES), 1)
    srow = lax.broadcasted_iota(jnp.int32, (ROW_STRIP, LANES), 0)

    def last_visible(r0, t):
        if t == 0:
            return slane < N_META
        c0 = (t - 1) * LANES
        if c0 + LANES - 1 <= r0:
            return True
        if c0 > r0 + ROW_STRIP - 1:
            return False
        return slane + c0 <= srow + r0

    k0 = pl.multiple_of(i * TQ, TQ)
    meta_aug = jnp.concatenate([kTm_ref[...], cam_ref[...], jnp.zeros((LANES - BIAS_ROWS, META_PAD), bf16)], axis=0)
    kTa = jnp.concatenate([meta_aug, keys_aug(k0)], axis=1)
    v = jnp.concatenate([vm_ref[...], v_ref[0, pl.ds(k0, TK), :]], axis=0)
    _attn_units([(j, 0, META_PAD + TK, kTa, v, last_visible) for j in range(2)], qas, *scr)

    outs = [acc_scr[j] / jnp.sum(l_scr[j], axis=1, keepdims=True) for j in range(2)]
    o_ref[0] = jnp.where(lane < HEAD_DIM, outs[0], outs[1])


def _attn_call(q, kTb, vb, caug, kTm, vm, caugm):
    nb, t, _ = q.shape
    npair = N_HEADS // 2
    return pl.pallas_call(
        _attn_kernel,
        grid=(nb, npair, t // TQ),
        in_specs=[pl.BlockSpec((1, TQ, LANES), lambda b, h, i: (b, i, h)),
                  pl.BlockSpec((1, LANES, t), lambda b, h, i: (b, h, 0)),
                  pl.BlockSpec((1, t, LANES), lambda b, h, i: (b, 0, h)),
                  pl.BlockSpec((1, BIAS_ROWS, t), lambda b, h, i: (b, 0, 0)),
                  pl.BlockSpec((LANES, META_PAD), lambda b, h, i: (h, 0)),
                  pl.BlockSpec((META_PAD, LANES), lambda b, h, i: (0, h)),
                  pl.BlockSpec((BIAS_ROWS, META_PAD), lambda b, h, i: (0, 0))],
        out_specs=pl.BlockSpec((1, TQ, LANES), lambda b, h, i: (b, i, h)),
        out_shape=jax.ShapeDtypeStruct((nb, t, D_ATT), f32),
        scratch_shapes=[pltpu.VMEM((2, TQ, 2 * TK), f32), pltpu.VMEM((2, TQ, 2 * TK), bf16),
                        pltpu.VMEM((2, TQ, LANES), f32), pltpu.VMEM((2, TQ, LANES), f32),
                        pltpu.VMEM((2, TQ, LANES), f32)],
        compiler_params=pltpu.CompilerParams(dimension_semantics=("arbitrary", "arbitrary", "arbitrary"),
                                             vmem_limit_bytes=VMEM_LIMIT),
        name="prompt_attention",
    )(q, kTb, vb, caug, kTm, vm, caugm)


def _ffn_kernel(x_ref, att_ref, rn_ref, natt_ref, woa_ref, wor_ref, nffn_ref, wup_ref, fw_ref, fb_ref,
                wdn_ref, nfin_ref, gp0_ref, y_ref, gplast_ref, gp_scr):
    i = pl.program_id(1)

    @pl.when(i == 0)
    def _():
        gp_scr[0:SUBLANES, :] = gp0_ref[...]

    an = _rms(att_ref[0], natt_ref[...]).astype(bf16)
    x1 = x_ref[0] + _dot(an, woa_ref[...]) + _dot(rn_ref[0], wor_ref[...])
    hn = _rms(x1, nffn_ref[...]).astype(bf16)
    down = None
    for c0, wd in FF_CHUNKS:
        cols = slice(c0, c0 + wd)
        gp = _dot(hn, wup_ref[:, cols])
        val = _dot(hn, wup_ref[:, D_FF + c0:D_FF + c0 + wd])
        gp_scr[SUBLANES:SUBLANES + TM, cols] = gp
        fw = fw_ref[:, cols]
        g = (fb_ref[:, cols] + fw[0:1] * gp_scr[6:6 + TM, cols] + fw[1:2] * gp_scr[7:7 + TM, cols]
             + fw[2:3] * gp)
        act = (_gelu(g) * val).astype(bf16)
        part = _dot(act, wdn_ref[cols, :])
        down = part if down is None else down + part
    last = gp_scr[TM:TM + SUBLANES, :]
    gp_scr[0:SUBLANES, :] = last
    gplast_ref[0] = last
    y_ref[0] = _rms(x1 + down, nfin_ref[...])


def _ffn_call(x_real, att, rn, natt, woa, wor, nffn, wup, fw, fb, wdn, nfin, gp0):
    nb, t, _ = x_real.shape
    const = lambda shape: pl.BlockSpec(shape, lambda b, i: (0,) * len(shape), pipeline_mode=pl.Buffered(1))
    row_blk = lambda w: pl.BlockSpec((1, TM, w), lambda b, i: (b, i, 0))
    return pl.pallas_call(
        _ffn_kernel,
        grid=(nb, t // TM),
        in_specs=[row_blk(D_MODEL), row_blk(D_ATT), row_blk(D_RNN), const((1, D_ATT)),
                  const((D_ATT, D_MODEL)), const((D_RNN, D_MODEL)), const((1, D_MODEL)),
                  const((D_MODEL, 2 * D_FF)), const((3, D_FF)), const((1, D_FF)),
                  const((D_FF, D_MODEL)), const((1, D_MODEL)), const((SUBLANES, D_FF))],
        out_specs=[row_blk(D_MODEL), pl.BlockSpec((1, SUBLANES, D_FF), lambda b, i: (b, 0, 0))],
        out_shape=[jax.ShapeDtypeStruct((nb, t, D_MODEL), f32),
                   jax.ShapeDtypeStruct((nb, SUBLANES, D_FF), f32)],
        scratch_shapes=[pltpu.VMEM((TM + SUBLANES, D_FF), f32)],
        compiler_params=pltpu.CompilerParams(dimension_semantics=("arbitrary", "arbitrary"),
                                             vmem_limit_bytes=VMEM_LIMIT),
        name="outproj_ffn",
    )(x_real, att, rn, natt, woa, wor, nffn, wup, fw, fb, wdn, nfin, gp0)


def _small_pre_kernel(x_ref, nmix_ref, w_ref, bfc_ref, bfr_ref, cw_ref, cb_ref, wg_ref, bga_ref, bgx_ref,
                      lam_ref, nrnn_ref, hs0_ref, cs0_ref, cs1_ref, cs2_ref,
                      qs_ref, ks_ref, vs_ref, lfs_ref, hs_ref, xrs_ref,
                      kTm_ref, vTm_ref, km_ref, vm_ref, lfTm_ref, cTm_ref, cam_ref, attm_ref, hm_ref, xrm_ref, rn_ref):
    m0 = META_PAD
    hn = _rms(x_ref[...], nmix_ref[...]).astype(bf16)
    z = _dot(hn, w_ref[...])
    cw = cw_ref[...]
    lam = lam_ref[...]

    zs = z[m0:, :]
    qs_ref[...] = zs[:, 0:512]
    ks_ref[...] = zs[:, 512:1024]
    vs_ref[...] = zs[:, 1024:1536]
    lfs_ref[...] = _log_sigmoid(zs[:, ZF_COL:ZF_COL + N_HEADS] + bfr_ref[...])
    xrs = zs[:, 1536:2048]
    xrs_ref[...] = xrs
    xcs = cb_ref[...] + cw[0:1] * cs0_ref[...] + cw[1:2] * cs1_ref[...] + cw[2:3] * cs2_ref[...] + cw[3:4] * xrs
    a_s, u_s = _gates(xcs, wg_ref, bga_ref[...], bgx_ref[...], lam)
    h_s = a_s * hs0_ref[...] + u_s
    hs_ref[...] = h_s
    rn_ref[m0:, :] = _rms(h_s * _gelu(zs[:, 2048:2560]), nrnn_ref[...])

    zm = z[0:m0, :]
    km = zm[:, 512:1024]
    vm = zm[:, 1024:1536]
    kTm = km.T
    kTm_ref[...] = kTm
    vTm_ref[...] = vm.T
    km_ref[...] = km
    vm_ref[...] = vm
    lfT = _log_sigmoid(zm[:, ZF_COL:ZF_COL + LANES].T[0:N_HEADS, :] + bfc_ref[...])
    lfTm_ref[...] = lfT
    cT = _cumsum_lanes(lfT)
    cTm_ref[...] = cT
    cam_ref[...] = _bias_parts(cT)

    xrm = zm[:, 1536:2048]
    xrm_ref[...] = xrm[SUBLANES:2 * SUBLANES, :]
    xcm = (cb_ref[...] + cw[0:1] * _shift_rows(xrm, 3, 0.0) + cw[1:2] * _shift_rows(xrm, 2, 0.0)
           + cw[2:3] * _shift_rows(xrm, 1, 0.0) + cw[3:4] * xrm)
    a_m, u_m = _gates(xcm[0:N_META], wg_ref, bga_ref[...], bgx_ref[...], lam)
    _, h_m = _scan_rows(a_m, u_m)
    hm_ref[...] = h_m[SUBLANES:2 * SUBLANES, :]
    rn_ref[0:N_META, :] = _rms(h_m * _gelu(zm[0:N_META, 2048:2560]), nrnn_ref[...])
    rn_ref[N_META:m0, :] = jnp.zeros((m0 - N_META, D_RNN), f32)

    qm = zm[:, 0:512].astype(bf16)
    kTb = kTm.astype(bf16)
    vb = vm.astype(bf16)
    lane = lax.broadcasted_iota(jnp.int32, (m0, LANES), 1)
    row = lax.broadcasted_iota(jnp.int32, (m0, LANES), 0)
    mask = (lane <= row) & (lane < N_META)
    for hp in range(N_HEADS // 2):
        sl = slice(hp * LANES, (hp + 1) * LANES)
        qp = qm[:, sl]
        outs = []
        for j in range(2):
            h = 2 * hp + j
            qj = jnp.where((lane >= j * HEAD_DIM) & (lane < (j + 1) * HEAD_DIM), qp, jnp.zeros_like(qp))
            s = _dot(qj, kTb[sl, :]) - cT[h:h + 1, :]
            s = jnp.where(mask, s, -jnp.inf)
            p = jnp.exp(s - jnp.max(s, axis=1, keepdims=True))
            o = _dot(p.astype(bf16), vb[:, sl]) / jnp.sum(p, axis=1, keepdims=True)
            outs.append(o)
        attm_ref[:, sl] = jnp.where(lane < HEAD_DIM, outs[0], outs[1])


def _small_pre_call(xs, nmix, w_cat, bf_col, bf_row, cw, cb, wg, bga, bgx, lam, nrnn, hs0, cs0, cs1, cs2):
    ns = SMALL_ROWS - META_PAD
    shp = jax.ShapeDtypeStruct
    return pl.pallas_call(
        _small_pre_kernel,
        out_shape=[shp((ns, D_ATT), f32), shp((ns, D_ATT), f32), shp((ns, D_ATT), f32), shp((ns, N_HEADS), f32),
                   shp((ns, D_RNN), f32), shp((ns, D_RNN), f32),
                   shp((D_ATT, META_PAD), f32), shp((D_ATT, META_PAD), f32),
                   shp((META_PAD, D_ATT), f32), shp((META_PAD, D_ATT), f32),
                   shp((N_HEADS, META_PAD), f32), shp((N_HEADS, META_PAD), f32), shp((BIAS_ROWS, META_PAD), bf16),
                   shp((META_PAD, D_ATT), f32), shp((SUBLANES, D_RNN), f32), shp((SUBLANES, D_RNN), f32),
                   shp((SMALL_ROWS, D_RNN), f32)],
        compiler_params=pltpu.CompilerParams(vmem_limit_bytes=VMEM_LIMIT),
        name="small_pre",
    )(xs, nmix, w_cat, bf_col, bf_row, cw, cb, wg, bga, bgx, lam, nrnn, hs0, cs0, cs1, cs2)


def _small_post_kernel(x_ref, att_ref, rn_ref, natt_ref, woa_ref, wor_ref, nffn_ref, wup_ref, fw_ref, fb_ref,
                       wdn_ref, nfin_ref, f0_ref, f1_ref, ys_ref, gps_ref, gpm_ref):
    m0 = META_PAD
    an = _rms(att_ref[...], natt_ref[...]).astype(bf16)
    x1 = x_ref[...] + _dot(an, woa_ref[...]) + _dot(rn_ref[...].astype(bf16), wor_ref[...])
    hn = _rms(x1, nffn_ref[...]).astype(bf16)
    up = _dot(hn, wup_ref[...])
    gp = up[:, 0:D_FF]
    val = up[:, D_FF:]
    fw = fw_ref[...]
    gpm = gp[0:m0]
    gm = fb_ref[...] + fw[0:1] * _shift_rows(gpm, 2, 0.0) + fw[1:2] * _shift_rows(gpm, 1, 0.0) + fw[2:3] * gpm
    gps = gp[m0:]
    gs = fb_ref[...] + fw[0:1] * f0_ref[...] + fw[1:2] * f1_ref[...] + fw[2:3] * gps
    gps_ref[...] = gps
    gpm_ref[...] = gpm[SUBLANES:2 * SUBLANES, :]
    g = jnp.concatenate([gm, gs], axis=0)
    act = (_gelu(g) * val).astype(bf16)
    x2 = x1 + _dot(act, wdn_ref[...])
    ys_ref[...] = _rms(x2[m0:], nfin_ref[...])


def _small_post_call(xs, att, rn, natt, woa, wor, nffn, wup, fw, fb, wdn, nfin, f0, f1):
    ns = SMALL_ROWS - META_PAD
    shp = jax.ShapeDtypeStruct
    return pl.pallas_call(
        _small_post_kernel,
        out_shape=[shp((ns, D_MODEL), f32), shp((ns, D_FF), f32), shp((SUBLANES, D_FF), f32)],
        compiler_params=pltpu.CompilerParams(vmem_limit_bytes=VMEM_LIMIT),
        name="small_post",
    )(xs, att, rn, natt, woa, wor, nffn, wup, fw, fb, wdn, nfin, f0, f1)


def _paged_kernel(pt_ref, qbc_ref, q_ref, kn_ref, vn_ref, lfn_ref, ck_ref, cv_ref, cl_ref, o_ref,
                  buf, lbuf, s_scr, acc_scr, sem, lsem, *, n_seq, n_pages):
    b = pl.program_id(0)
    n_chunks = n_pages // PAGES_PER_CHUNK
    per_seq = 2 * n_chunks

    def chunk_copies(seq, c, slot):
        src = ck_ref if c < n_chunks else cv_ref
        p0 = (c % n_chunks) * PAGES_PER_CHUNK
        return [pltpu.make_async_copy(src.at[pt_ref[seq, p0 + pg]], buf.at[slot, pg], sem.at[slot])
                for pg in range(PAGES_PER_CHUNK)]

    def logf_copies(seq, slot):
        return [pltpu.make_async_copy(cl_ref.at[pt_ref[seq, pg]], lbuf.at[slot, pg], lsem.at[slot])
                for pg in range(n_pages)]

    lslot = b % 2
    nxt = jnp.minimum(b + 1, n_seq - 1)

    @pl.when(b == 0)
    def _():
        for cp in logf_copies(0, 0):
            cp.start()
        for c in range(N_SLOTS - 1):
            for cp in chunk_copies(0, c, c):
                cp.start()

    @pl.when(b + 1 < n_seq)
    def _():
        for cp in logf_copies(nxt, 1 - lslot):
            cp.start()

    acc_scr[...] = jnp.zeros(acc_scr.shape, f32)

    for c in range(per_seq):
        ahead = c + N_SLOTS - 1
        if ahead < per_seq:
            for cp in chunk_copies(b, ahead, ahead % N_SLOTS):
                cp.start()
        else:
            @pl.when(b + 1 < n_seq)
            def _():
                for cp in chunk_copies(nxt, ahead - per_seq, ahead % N_SLOTS):
                    cp.start()
        slot = c % N_SLOTS
        for cp in chunk_copies(b, c, slot):
            cp.wait()

        if c < n_chunks:
            for h in range(N_HEADS):
                qh = qbc_ref[0, h]
                for pg in range(PAGES_PER_CHUNK):
                    red = jnp.sum(buf[slot, pg, h] * qh, axis=0, keepdims=True)
                    s_scr[c * PAGES_PER_CHUNK + pg, h:h + 1, :] = red
        else:
            for h in range(N_HEADS):
                a = jnp.zeros((HEAD_DIM, PAGE), f32)
                for pg in range(PAGES_PER_CHUNK):
                    page = (c - n_chunks) * PAGES_PER_CHUNK + pg
                    a = a + buf[slot, pg, h] * s_scr[page, h:h + 1, :]
                acc_scr[h] += a

        if c == n_chunks - 1:
            for cp in logf_copies(b, lslot):
                cp.wait()
            rows = n_pages * N_HEADS
            lf = lbuf[lslot].reshape(rows, PAGE)
            cw = _cumsum_lanes(lf)
            tot = jnp.broadcast_to(cw[:, PAGE - 1:PAGE], (rows, PAGE))
            ridx = lax.broadcasted_iota(jnp.int32, (rows, PAGE), 0)
            off = tot
            sh = N_HEADS
            while sh < rows:
                off = off + jnp.where(ridx >= sh, pltpu.roll(off, sh, 0), 0.0)
                sh *= 2
            cfull = cw + (off - tot)
            s3 = (s_scr[...].reshape(rows, PAGE) - cfull).reshape(n_pages, N_HEADS, PAGE)
            c_past = off[rows - N_HEADS:rows, 0:1]
            s_new = jnp.sum(q_ref[0] * kn_ref[0], axis=1, keepdims=True) - (c_past + lfn_ref[0])
            m = jnp.max(jnp.max(s3, axis=0), axis=1, keepdims=True)
            m = jnp.maximum(m, s_new)
            p3 = jnp.exp(s3 - m[None])
            p_new = jnp.exp(s_new - m)
            l = jnp.sum(jnp.sum(p3, axis=0), axis=1, keepdims=True) + p_new
            s_scr[...] = p3

    num = jnp.sum(acc_scr[...], axis=2) + p_new * vn_ref[0]
    o_ref[0] = num / l


def _paged_call(page_table, qbc, q3, kn3, vn3, lfn3, ckT, cvT, clT):
    n_seq, n_pages = page_table.shape
    blk3 = lambda w: pl.BlockSpec((1, N_HEADS, w), lambda b, pt: (b, 0, 0))
    grid_spec = pltpu.PrefetchScalarGridSpec(
        num_scalar_prefetch=1,
        grid=(n_seq,),
        in_specs=[pl.BlockSpec((1, N_HEADS, HEAD_DIM, PAGE), lambda b, pt: (b, 0, 0, 0)),
                  blk3(HEAD_DIM), blk3(HEAD_DIM), blk3(HEAD_DIM), blk3(1),
                  pl.BlockSpec(memory_space=pl.ANY), pl.BlockSpec(memory_space=pl.ANY),
                  pl.BlockSpec(memory_space=pl.ANY)],
        out_specs=blk3(HEAD_DIM),
        scratch_shapes=[pltpu.VMEM((N_SLOTS, PAGES_PER_CHUNK, N_HEADS, HEAD_DIM, PAGE), f32),
                        pltpu.VMEM((2, n_pages, N_HEADS, PAGE), f32),
                        pltpu.VMEM((n_pages, N_HEADS, PAGE), f32),
                        pltpu.VMEM((N_HEADS, HEAD_DIM, PAGE), f32),
                        pltpu.SemaphoreType.DMA((N_SLOTS,)), pltpu.SemaphoreType.DMA((2,))])
    return pl.pallas_call(
        functools.partial(_paged_kernel, n_seq=n_seq, n_pages=n_pages),
        grid_spec=grid_spec,
        out_shape=jax.ShapeDtypeStruct((n_seq, N_HEADS, HEAD_DIM), f32),
        compiler_params=pltpu.CompilerParams(dimension_semantics=("arbitrary",), vmem_limit_bytes=VMEM_LIMIT),
        name="paged_attention",
    )(page_table, qbc, q3, kn3, vn3, lfn3, ckT, cvT, clT)


def _block_diag(w):
    n, r, c = w.shape
    eye = jnp.eye(n, dtype=w.dtype)
    return (eye[:, None, :, None] * w[:, :, None, :]).reshape(n * r, n * c)


def kernel(x_prompt, x_sample, cache_k, cache_v, cache_logf, state_rnn_h, state_rnn_conv, state_ffn_conv, page_table, meta_tokens, norm_mix, w_in, b_forget, rnn_conv_w, rnn_conv_b, w_rg_a, b_rg_a, w_rg_x, b_rg_x, rg_lambda, norm_att_out, norm_rnn_out, w_out, norm_ffn, w_ffn_up, ffn_conv_w, ffn_conv_b, w_ffn_down, norm_final):
    nb, seq, _ = x_prompt.shape
    db = x_sample.shape[0]
    assert norm_mix.shape[0] == 1 and x_sample.shape[1] == 1 and seq % TM == 0 and seq % TQ == 0

    w = w_in[0]
    wf = jnp.pad(w[:, 3 * D_ATT:3 * D_ATT + N_HEADS], ((0, 0), (0, LANES - N_HEADS)))
    w_cat = jnp.concatenate([w[:, :D_ATT] * ATT_SCALE, w[:, D_ATT:3 * D_ATT], w[:, 3 * D_ATT + N_HEADS:], wf],
                            axis=1).astype(bf16)
    bda = _block_diag(w_rg_a[0])
    bdx = _block_diag(w_rg_x[0])
    wg = jnp.stack([jnp.concatenate([bda[:256, :256], bdx[:256, :256]], axis=1),
                    jnp.concatenate([bda[256:, 256:], bdx[256:, 256:]], axis=1)]).astype(bf16)
    woa = w_out[0, :D_ATT].astype(bf16)
    wor = w_out[0, D_ATT:].astype(bf16)
    wup = w_ffn_up[0].astype(bf16)
    wdn = w_ffn_down[0].astype(bf16)
    bf_col = b_forget[0][:, None]
    bf_row = b_forget
    nfin = norm_final[None, :]
    mix_args = (norm_mix, w_cat, bf_col)
    rnn_args = (rnn_conv_w[0], rnn_conv_b, wg, b_rg_a, b_rg_x, rg_lambda, norm_rnn_out)
    post_args = (norm_att_out, woa, wor, norm_ffn, wup, ffn_conv_w[0], ffn_conv_b, wdn, nfin)

    xs = jnp.concatenate([meta_tokens, jnp.zeros((META_PAD - N_META, D_MODEL), f32), x_sample[:, 0, :]], axis=0)
    (q_s, k_s, v_s, lf_s, h_s, xr_s, kTm, vTm, km, vm, lfTm, cTm, caug_m, att_m, h_m, xr_m, rn_small) = _small_pre_call(
        xs, *mix_args, bf_row, *rnn_args, state_rnn_h[0],
        state_rnn_conv[0, :, 0], state_rnn_conv[0, :, 1], state_rnn_conv[0, :, 2])

    ckT = jnp.transpose(cache_k[0], (0, 2, 3, 1))
    cvT = jnp.transpose(cache_v[0], (0, 2, 3, 1))
    clT = jnp.transpose(cache_logf[0], (0, 2, 1))
    q3 = q_s.reshape(db, N_HEADS, HEAD_DIM)
    qbc = jnp.broadcast_to(q3[:, :, :, None], (db, N_HEADS, HEAD_DIM, PAGE))
    att_s = _paged_call(page_table, qbc, q3, k_s.reshape(db, N_HEADS, HEAD_DIM),
                        v_s.reshape(db, N_HEADS, HEAD_DIM), lf_s[:, :, None], ckT, cvT, clT)

    att_small = jnp.concatenate([att_m, att_s.reshape(db, D_ATT)], axis=0)
    y_s, gp_s, gp_m = _small_post_call(xs, att_small, rn_small, *post_args,
                                       state_ffn_conv[0, :, 0], state_ffn_conv[0, :, 1])

    (q, kT, vT, kTb, vb, lfT, caug, rn, h_last, xr_last) = _inproj_call(
        x_prompt, *mix_args, *rnn_args, h_m, xr_m, cTm)
    att = _attn_call(q, kTb, vb, caug, kTm.astype(bf16), vm.astype(bf16), caug_m)
    y_prompt, gp_last = _ffn_call(x_prompt, att, rn, *post_args, gp_m)

    def with_meta(meta_t, real_t):
        c = meta_t.shape[0]
        return jnp.concatenate([jnp.broadcast_to(meta_t[None, :, :N_META], (nb, c, N_META)), real_t], axis=2)

    t_all = N_META + seq
    k_prompt = jnp.transpose(with_meta(kTm, kT).reshape(nb, N_HEADS, HEAD_DIM, t_all), (0, 3, 1, 2))[None]
    v_prompt = jnp.transpose(with_meta(vTm, vT).reshape(nb, N_HEADS, HEAD_DIM, t_all), (0, 3, 1, 2))[None]
    logf_prompt = jnp.transpose(with_meta(lfTm, lfT), (0, 2, 1))[None]
    rnn_h_prompt = h_last[:, 0, :][None]
    rnn_conv_prompt = xr_last[:, SUBLANES - 3:, :][None]
    ffn_conv_prompt = gp_last[:, SUBLANES - 2:, :][None]

    k_sample = k_s.reshape(1, db, 1, N_HEADS, HEAD_DIM)
    v_sample = v_s.reshape(1, db, 1, N_HEADS, HEAD_DIM)
    logf_sample = lf_s.reshape(1, db, 1, N_HEADS)
    rnn_h_sample = h_s[None]
    rnn_conv_sample = jnp.stack([state_rnn_conv[0, :, 1], state_rnn_conv[0, :, 2], xr_s], axis=1)[None]
    ffn_conv_sample = jnp.stack([state_ffn_conv[0, :, 1], gp_s], axis=1)[None]
    y_sample = y_s[:, None, :]

    return (y_prompt, y_sample, k_prompt, v_prompt, logf_prompt, rnn_h_prompt, rnn_conv_prompt, ffn_conv_prompt,
            k_sample, v_sample, logf_sample, rnn_h_sample, rnn_conv_sample, ffn_conv_sample)
```

```python
import functools

import jax
import jax.numpy as jnp
from jax import lax
from jax.experimental import pallas as pl
from jax.experimental.pallas import tpu as pltpu

D_MODEL = 1024
N_META = 16
D_ATT = 512
D_RNN = 512
HEAD_DIM = 64
N_HEADS = 8
N_RNN_BLOCKS = 8
RG_C = 8.0
D_FF = 2816
EPS = 1e-6
ATT_SCALE = HEAD_DIM ** -0.5
PAGE = 128

LANES = 128
SUBLANES = 8
ZF_COL = 5 * 512
W_IN_COLS = ZF_COL + LANES
META_PAD = 128
SMALL_ROWS = META_PAD + 32
VMEM_LIMIT = 60 * 1024 * 1024

TM = 512
TQ = 512
TK = 512
ROW_STRIP = 32
BIAS_ROWS = 32
LOG2E = 1.4426950408889634
FF_CHUNKS = ((0, 1536), (1536, 1280))
PAGES_PER_CHUNK = 8
N_SLOTS = 8

bf16 = jnp.bfloat16
f32 = jnp.float32


def _rms(x, g):
    return x * lax.rsqrt(jnp.mean(x * x, axis=-1, keepdims=True) + EPS) * g


def _gelu(x):
    return x * (0.5 * (1.0 + jnp.tanh(0.7978845608028654 * (x + 0.044715 * (x * x * x)))))


def _sigmoid(x):
    return 0.5 * (1.0 + jnp.tanh(0.5 * x))


def _softplus(x):
    return jnp.maximum(x, 0.0) + jnp.log1p(jnp.exp(-jnp.abs(x)))


def _log_sigmoid(x):
    return -_softplus(-x)


def _dot(a, b):
    return jnp.dot(a, b, preferred_element_type=f32)


def _shift_rows(x, s, fill):
    row = lax.broadcasted_iota(jnp.int32, x.shape, 0)
    return jnp.where(row >= s, pltpu.roll(x, s, 0), fill)


def _cumsum_lanes(x):
    n = x.shape[-1]
    lane = lax.broadcasted_iota(jnp.int32, x.shape, x.ndim - 1)
    s = 1
    while s < n:
        x = x + jnp.where(lane >= s, pltpu.roll(x, s, x.ndim - 1), 0.0)
        s *= 2
    return x


def _scan_rows(a, u):
    n = a.shape[0]
    s = 1
    while s < n:
        a_sh = _shift_rows(a, s, 1.0)
        u_sh = _shift_rows(u, s, 0.0)
        u = a * u_sh + u
        a = a * a_sh
        s *= 2
    return a, u


def _gates(xc, wg_ref, bga, bgx, lam):
    xcb = xc.astype(bf16)
    half = D_RNN // 2
    g0 = _dot(xcb[:, :half], wg_ref[0])
    g1 = _dot(xcb[:, half:], wg_ref[1])
    ga = jnp.concatenate([g0[:, :half], g1[:, :half]], axis=1) + bga
    gx = jnp.concatenate([g0[:, half:], g1[:, half:]], axis=1) + bgx
    r = _sigmoid(ga)
    ig = _sigmoid(gx)
    log_a = (-RG_C) * r * _softplus(-lam)
    a = jnp.exp(log_a)
    u = jnp.sqrt(-jnp.tanh(log_a) * (1.0 + a * a)) * (ig * xc)
    return a, u


def _inproj_kernel(x_ref, nmix_ref, w_ref, bfc_ref, cw_ref, cb_ref, wg_ref, bga_ref, bgx_ref, lam_ref,
                   nrnn_ref, h0_ref, xr0_ref, c0_ref,
                   q_ref, kT_ref, vT_ref, kTb_ref, vb_ref, lfT_ref, ca_ref, rn_ref, hlast_ref, xrlast_ref,
                   xp_scr, hc_scr, cc_scr):
    i = pl.program_id(1)

    @pl.when(i == 0)
    def _():
        xp_scr[0:SUBLANES, :] = xr0_ref[...]
        hc_scr[...] = h0_ref[SUBLANES - 1:SUBLANES, :]
        cc_scr[...] = c0_ref[:, N_META - 1:N_META]

    hn = _rms(x_ref[0], nmix_ref[...]).astype(bf16)
    zr = _dot(hn, w_ref[:, 1536:2560])
    xr = zr[:, 0:512]
    gr = zr[:, 512:1024]
    zqk = _dot(hn, w_ref[:, 0:1024])

    xp_scr[SUBLANES:SUBLANES + TM, :] = xr
    cw = cw_ref[...]
    xc = (cb_ref[...] + cw[0:1] * xp_scr[5:5 + TM, :] + cw[1:2] * xp_scr[6:6 + TM, :]
          + cw[2:3] * xp_scr[7:7 + TM, :] + cw[3:4] * xr)
    xp_scr[0:SUBLANES, :] = xr[TM - SUBLANES:TM, :]
    xrlast_ref[0] = xr[TM - SUBLANES:TM, :]

    a, u = _gates(xc, wg_ref, bga_ref[...], bgx_ref[...], lam_ref[...])
    zv = _dot(hn, w_ref[:, 1024:1536])
    zf = _dot(hn, w_ref[:, ZF_COL:ZF_COL + LANES])

    q_ref[0] = (zqk[:, 0:512] * LOG2E).astype(bf16)
    kT = zqk[:, 512:1024].T
    kT_ref[0] = kT
    kTb_ref[0] = kT.astype(bf16)

    hl = hc_scr[...]
    hs = []
    for r0 in range(0, TM, SUBLANES):
        ag, ug = _scan_rows(a[r0:r0 + SUBLANES], u[r0:r0 + SUBLANES])
        hg = ag * hl + ug
        hs.append(hg)
        hl = hg[SUBLANES - 1:SUBLANES, :]
    hc_scr[...] = hl
    hlast_ref[0] = hl

    vT_ref[0] = zv.T
    vb_ref[0] = zv.astype(bf16)
    lfT = _log_sigmoid(zf.T[0:N_HEADS, :] + bfc_ref[...])
    lfT_ref[0] = lfT
    cum = _cumsum_lanes(lfT) + cc_scr[...]
    ca_ref[0] = _bias_parts(cum)
    cc_scr[...] = cum[:, TM - 1:TM]

    y = jnp.concatenate(hs, axis=0) * _gelu(gr)
    rn_ref[0] = _rms(y, nrnn_ref[...]).astype(bf16)


def _inproj_call(x_real, nmix, w_cat, bf_col, cw, cb, wg, bga, bgx, lam, nrnn, h0, xr0, c0):
    nb, t, _ = x_real.shape
    nt = t // TM
    const = lambda shape: pl.BlockSpec(shape, lambda b, i: (0,) * len(shape), pipeline_mode=pl.Buffered(1))
    row_blk = lambda w: pl.BlockSpec((1, TM, w), lambda b, i: (b, i, 0))
    col_blk = lambda r: pl.BlockSpec((1, r, TM), lambda b, i: (b, 0, i))
    per_b = lambda r, w: pl.BlockSpec((1, r, w), lambda b, i: (b, 0, 0))
    return pl.pallas_call(
        _inproj_kernel,
        grid=(nb, nt),
        in_specs=[row_blk(D_MODEL), const((1, D_MODEL)), const((D_MODEL, W_IN_COLS)), const((N_HEADS, 1)),
                  const((4, D_RNN)), const((1, D_RNN)), const((2, 256, 512)), const((1, D_RNN)),
                  const((1, D_RNN)), const((1, D_RNN)), const((1, D_RNN)),
                  const((SUBLANES, D_RNN)), const((SUBLANES, D_RNN)), const((N_HEADS, LANES))],
        out_specs=[row_blk(D_ATT), col_blk(D_ATT), col_blk(D_ATT), col_blk(D_ATT), row_blk(D_ATT),
                   col_blk(N_HEADS), col_blk(BIAS_ROWS), row_blk(D_RNN), per_b(1, D_RNN), per_b(SUBLANES, D_RNN)],
        out_shape=[jax.ShapeDtypeStruct((nb, t, D_ATT), bf16),
                   jax.ShapeDtypeStruct((nb, D_ATT, t), f32),
                   jax.ShapeDtypeStruct((nb, D_ATT, t), f32),
                   jax.ShapeDtypeStruct((nb, D_ATT, t), bf16),
                   jax.ShapeDtypeStruct((nb, t, D_ATT), bf16),
                   jax.ShapeDtypeStruct((nb, N_HEADS, t), f32),
                   jax.ShapeDtypeStruct((nb, BIAS_ROWS, t), bf16),
                   jax.ShapeDtypeStruct((nb, t, D_RNN), bf16),
                   jax.ShapeDtypeStruct((nb, 1, D_RNN), f32),
                   jax.ShapeDtypeStruct((nb, SUBLANES, D_RNN), f32)],
        scratch_shapes=[pltpu.VMEM((TM + SUBLANES, D_RNN), f32),
                        pltpu.VMEM((1, D_RNN), f32), pltpu.VMEM((N_HEADS, 1), f32)],
        compiler_params=pltpu.CompilerParams(dimension_semantics=("arbitrary", "arbitrary"),
                                             vmem_limit_bytes=VMEM_LIMIT),
        name="inproj_rglru",
    )(x_real, nmix, w_cat, bf_col, cw, cb, wg, bga, bgx, lam, nrnn, h0, xr0, c0)


def _bias_parts(c):
    c2 = c * (-LOG2E)
    hi = c2.astype(bf16).astype(f32)
    r1 = c2 - hi
    mid = r1.astype(bf16).astype(f32)
    lo = (r1 - mid).astype(bf16).astype(f32)
    return jnp.concatenate([hi, mid, lo, jnp.zeros_like(hi)], axis=0).astype(bf16)


def _unit_softmax(j, c0, width, visible, s_scr, p_scr, m_scr, l_scr, acc_scr):
    for r0 in range(0, TQ, ROW_STRIP):
        rows = slice(r0, r0 + ROW_STRIP)
        st = []
        for t in range(width // LANES):
            vis = visible(r0, t)
            if vis is False:
                st.append(None)
                continue
            x = s_scr[j, rows, c0 + t * LANES:c0 + (t + 1) * LANES]
            st.append(x if vis is True else jnp.where(vis, x, -jnp.inf))
        live = [x for x in st if x is not None]
        mx = live[0]
        for x in live[1:]:
            mx = jnp.maximum(mx, x)
        m_old = m_scr[j, rows, :]
        m_new = jnp.maximum(m_old, jnp.max(mx, axis=1, keepdims=True))
        alpha = jnp.exp2(m_old - m_new)
        psum = None
        for t, x in enumerate(st):
            cols = slice(c0 + t * LANES, c0 + (t + 1) * LANES)
            if x is None:
                p_scr[j, rows, cols] = jnp.zeros((ROW_STRIP, LANES), bf16)
                continue
            p = jnp.exp2(x - m_new)
            psum = p if psum is None else psum + p
            p_scr[j, rows, cols] = p.astype(bf16)
        m_scr[j, rows, :] = m_new
        l_scr[j, rows, :] = alpha * l_scr[j, rows, :] + psum
        acc_scr[j, rows, :] = alpha * acc_scr[j, rows, :]


def _attn_units(units, qas, s_scr, p_scr, m_scr, l_scr, acc_scr):
    def scores(u):
        j, c0, width, kTa, _, _ = u
        s_scr[j, :, c0:c0 + width] = _dot(qas[j], kTa)

    def values(u):
        j, c0, width, _, v, _ = u
        acc_scr[j] += _dot(p_scr[j, :, c0:c0 + width], v)

    for u in units[:2]:
        scores(u)
    for n, u in enumerate(units):
        j, c0, width, _, _, visible = u
        _unit_softmax(j, c0, width, visible, s_scr, p_scr, m_scr, l_scr, acc_scr)
        if n + 2 < len(units):
            scores(units[n + 2])
        values(u)


def _attn_kernel(q_ref, kT_ref, v_ref, ca_ref, kTm_ref, vm_ref, cam_ref, o_ref, s_scr, p_scr, m_scr, l_scr, acc_scr):
    hp = pl.program_id(1)
    i = pl.program_id(2)
    q = q_ref[0].astype(f32)
    lane = lax.broadcasted_iota(jnp.int32, (TQ, LANES), 1)
    qas = []
    for j in range(2):
        h = 2 * hp + j
        qj = jnp.where((lane >= j * HEAD_DIM) & (lane < (j + 1) * HEAD_DIM), q, 0.0)
        pick = jnp.where((lane == h) | (lane == h + N_HEADS) | (lane == h + 2 * N_HEADS), 1.0, 0.0)
        qas.append(jnp.concatenate([qj, pick], axis=1).astype(bf16))
    scr = (s_scr, p_scr, m_scr, l_scr, acc_scr)
    m_scr[...] = jnp.full(m_scr.shape, -jnp.inf, f32)
    l_scr[...] = jnp.zeros(l_scr.shape, f32)
    acc_scr[...] = jnp.zeros(acc_scr.shape, f32)

    def keys_aug(k0):
        pad = jnp.zeros((LANES - BIAS_ROWS, TK), bf16)
        return jnp.concatenate([kT_ref[0, :, pl.ds(k0, TK)], ca_ref[0, :, pl.ds(k0, TK)], pad], axis=0)

    def full_tiles(k0, n_tiles):
        units = []
        for a in range(n_tiles):
            ka = pl.multiple_of(k0 + a * TK, TK)
            kTa = keys_aug(ka)
            v = v_ref[0, pl.ds(ka, TK), :]
            units += [(j, a * TK, TK, kTa, v, lambda r0, t: True) for j in range(2)]
        _attn_units(units, qas, *scr)

    def body(kt, carry):
        full_tiles(kt * (2 * TK), 2)
        return carry

    lax.fori_loop(0, i // 2, body, 0)

    @pl.when(i % 2 == 1)
    def _():
        full_tiles((i - 1) * TK, 1)

    slane = lax.broadcasted_iota(jnp.int32, (ROW_STRIP, LANES), 1)
    srow = lax.broadcasted_iota(jnp.int32, (ROW_STRIP, LANES), 0)

    def last_visible(r0, t):
        if t == 0:
            return slane < N_META
        c0 = (t - 1) * LANES
        if c0 + LANES - 1 <= r0:
            return True
        if c0 > r0 + ROW_STRIP - 1:
            return False
        return slane + c0 <= srow + r0

    k0 = pl.multiple_of(i * TQ, TQ)
    meta_aug = jnp.concatenate([kTm_ref[...], cam_ref[...], jnp.zeros((LANES - BIAS_ROWS, META_PAD), bf16)], axis=0)
    kTa = jnp.concatenate([meta_aug, keys_aug(k0)], axis=1)
    v = jnp.concatenate([vm_ref[...], v_ref[0, pl.ds(k0, TK), :]], axis=0)
    _attn_units([(j, 0, META_PAD + TK, kTa, v, last_visible) for j in range(2)], qas, *scr)

    outs = [acc_scr[j] / jnp.sum(l_scr[j], axis=1, keepdims=True) for j in range(2)]
    o_ref[0] = jnp.where(lane < HEAD_DIM, outs[0], outs[1])


def _attn_call(q, kTb, vb, caug, kTm, vm, caugm):
    nb, t, _ = q.shape
    npair = N_HEADS // 2
    return pl.pallas_call(
        _attn_kernel,
        grid=(nb, npair, t // TQ),
        in_specs=[pl.BlockSpec((1, TQ, LANES), lambda b, h, i: (b, i, h)),
                  pl.BlockSpec((1, LANES, t), lambda b, h, i: (b, h, 0)),
                  pl.BlockSpec((1, t, LANES), lambda b, h, i: (b, 0, h)),
                  pl.BlockSpec((1, BIAS_ROWS, t), lambda b, h, i: (b, 0, 0)),
                  pl.BlockSpec((LANES, META_PAD), lambda b, h, i: (h, 0)),
                  pl.BlockSpec((META_PAD, LANES), lambda b, h, i: (0, h)),
                  pl.BlockSpec((BIAS_ROWS, META_PAD), lambda b, h, i: (0, 0))],
        out_specs=pl.BlockSpec((1, TQ, LANES), lambda b, h, i: (b, i, h)),
        out_shape=jax.ShapeDtypeStruct((nb, t, D_ATT), f32),
        scratch_shapes=[pltpu.VMEM((2, TQ, 2 * TK), f32), pltpu.VMEM((2, TQ, 2 * TK), bf16),
                        pltpu.VMEM((2, TQ, LANES), f32), pltpu.VMEM((2, TQ, LANES), f32),
                        pltpu.VMEM((2, TQ, LANES), f32)],
        compiler_params=pltpu.CompilerParams(dimension_semantics=("arbitrary", "arbitrary", "arbitrary"),
                                             vmem_limit_bytes=VMEM_LIMIT),
        name="prompt_attention",
    )(q, kTb, vb, caug, kTm, vm, caugm)


def _ffn_kernel(x_ref, att_ref, rn_ref, natt_ref, woa_ref, wor_ref, nffn_ref, wup_ref, fw_ref, fb_ref,
                wdn_ref, nfin_ref, gp0_ref, y_ref, gplast_ref, gp_scr):
    i = pl.program_id(1)

    @pl.when(i == 0)
    def _():
        gp_scr[0:SUBLANES, :] = gp0_ref[...]

    an = _rms(att_ref[0], natt_ref[...]).astype(bf16)
    x1 = x_ref[0] + _dot(an, woa_ref[...]) + _dot(rn_ref[0], wor_ref[...])
    hn = _rms(x1, nffn_ref[...]).astype(bf16)
    down = None
    for c0, wd in FF_CHUNKS:
        cols = slice(c0, c0 + wd)
        gp = _dot(hn, wup_ref[:, cols])
        val = _dot(hn, wup_ref[:, D_FF + c0:D_FF + c0 + wd])
        gp_scr[SUBLANES:SUBLANES + TM, cols] = gp
        fw = fw_ref[:, cols]
        g = (fb_ref[:, cols] + fw[0:1] * gp_scr[6:6 + TM, cols] + fw[1:2] * gp_scr[7:7 + TM, cols]
             + fw[2:3] * gp)
        act = (_gelu(g) * val).astype(bf16)
        part = _dot(act, wdn_ref[cols, :])
        down = part if down is None else down + part
    last = gp_scr[TM:TM + SUBLANES, :]
    gp_scr[0:SUBLANES, :] = last
    gplast_ref[0] = last
    y_ref[0] = _rms(x1 + down, nfin_ref[...])


def _ffn_call(x_real, att, rn, natt, woa, wor, nffn, wup, fw, fb, wdn, nfin, gp0):
    nb, t, _ = x_real.shape
    const = lambda shape: pl.BlockSpec(shape, lambda b, i: (0,) * len(shape), pipeline_mode=pl.Buffered(1))
    row_blk = lambda w: pl.BlockSpec((1, TM, w), lambda b, i: (b, i, 0))
    return pl.pallas_call(
        _ffn_kernel,
        grid=(nb, t // TM),
        in_specs=[row_blk(D_MODEL), row_blk(D_ATT), row_blk(D_RNN), const((1, D_ATT)),
                  const((D_ATT, D_MODEL)), const((D_RNN, D_MODEL)), const((1, D_MODEL)),
                  const((D_MODEL, 2 * D_FF)), const((3, D_FF)), const((1, D_FF)),
                  const((D_FF, D_MODEL)), const((1, D_MODEL)), const((SUBLANES, D_FF))],
        out_specs=[row_blk(D_MODEL), pl.BlockSpec((1, SUBLANES, D_FF), lambda b, i: (b, 0, 0))],
        out_shape=[jax.ShapeDtypeStruct((nb, t, D_MODEL), f32),
                   jax.ShapeDtypeStruct((nb, SUBLANES, D_FF), f32)],
        scratch_shapes=[pltpu.VMEM((TM + SUBLANES, D_FF), f32)],
        compiler_params=pltpu.CompilerParams(dimension_semantics=("arbitrary", "arbitrary"),
                                             vmem_limit_bytes=VMEM_LIMIT),
        name="outproj_ffn",
    )(x_real, att, rn, natt, woa, wor, nffn, wup, fw, fb, wdn, nfin, gp0)


def _small_pre_kernel(x_ref, nmix_ref, w_ref, bfc_ref, bfr_ref, cw_ref, cb_ref, wg_ref, bga_ref, bgx_ref,
                      lam_ref, nrnn_ref, hs0_ref, cs0_ref, cs1_ref, cs2_ref,
                      qs_ref, ks_ref, vs_ref, lfs_ref, hs_ref, xrs_ref,
                      kTm_ref, vTm_ref, km_ref, vm_ref, lfTm_ref, cTm_ref, cam_ref, attm_ref, hm_ref, xrm_ref, rn_ref):
    m0 = META_PAD
    hn = _rms(x_ref[...], nmix_ref[...]).astype(bf16)
    z = _dot(hn, w_ref[...])
    cw = cw_ref[...]
    lam = lam_ref[...]

    zs = z[m0:, :]
    qs_ref[...] = zs[:, 0:512]
    ks_ref[...] = zs[:, 512:1024]
    vs_ref[...] = zs[:, 1024:1536]
    lfs_ref[...] = _log_sigmoid(zs[:, ZF_COL:ZF_COL + N_HEADS] + bfr_ref[...])
    xrs = zs[:, 1536:2048]
    xrs_ref[...] = xrs
    xcs = cb_ref[...] + cw[0:1] * cs0_ref[...] + cw[1:2] * cs1_ref[...] + cw[2:3] * cs2_ref[...] + cw[3:4] * xrs
    a_s, u_s = _gates(xcs, wg_ref, bga_ref[...], bgx_ref[...], lam)
    h_s = a_s * hs0_ref[...] + u_s
    hs_ref[...] = h_s
    rn_ref[m0:, :] = _rms(h_s * _gelu(zs[:, 2048:2560]), nrnn_ref[...])

    zm = z[0:m0, :]
    km = zm[:, 512:1024]
    vm = zm[:, 1024:1536]
    kTm = km.T
    kTm_ref[...] = kTm
    vTm_ref[...] = vm.T
    km_ref[...] = km
    vm_ref[...] = vm
    lfT = _log_sigmoid(zm[:, ZF_COL:ZF_COL + LANES].T[0:N_HEADS, :] + bfc_ref[...])
    lfTm_ref[...] = lfT
    cT = _cumsum_lanes(lfT)
    cTm_ref[...] = cT
    cam_ref[...] = _bias_parts(cT)

    xrm = zm[:, 1536:2048]
    xrm_ref[...] = xrm[SUBLANES:2 * SUBLANES, :]
    xcm = (cb_ref[...] + cw[0:1] * _shift_rows(xrm, 3, 0.0) + cw[1:2] * _shift_rows(xrm, 2, 0.0)
           + cw[2:3] * _shift_rows(xrm, 1, 0.0) + cw[3:4] * xrm)
    a_m, u_m = _gates(xcm[0:N_META], wg_ref, bga_ref[...], bgx_ref[...], lam)
    _, h_m = _scan_rows(a_m, u_m)
    hm_ref[...] = h_m[SUBLANES:2 * SUBLANES, :]
    rn_ref[0:N_META, :] = _rms(h_m * _gelu(zm[0:N_META, 2048:2560]), nrnn_ref[...])
    rn_ref[N_META:m0, :] = jnp.zeros((m0 - N_META, D_RNN), f32)

    qm = zm[:, 0:512].astype(bf16)
    kTb = kTm.astype(bf16)
    vb = vm.astype(bf16)
    lane = lax.broadcasted_iota(jnp.int32, (m0, LANES), 1)
    row = lax.broadcasted_iota(jnp.int32, (m0, LANES), 0)
    mask = (lane <= row) & (lane < N_META)
    for hp in range(N_HEADS // 2):
        sl = slice(hp * LANES, (hp + 1) * LANES)
        qp = qm[:, sl]
        outs = []
        for j in range(2):
            h = 2 * hp + j
            qj = jnp.where((lane >= j * HEAD_DIM) & (lane < (j + 1) * HEAD_DIM), qp, jnp.zeros_like(qp))
            s = _dot(qj, kTb[sl, :]) - cT[h:h + 1, :]
            s = jnp.where(mask, s, -jnp.inf)
            p = jnp.exp(s - jnp.max(s, axis=1, keepdims=True))
            o = _dot(p.astype(bf16), vb[:, sl]) / jnp.sum(p, axis=1, keepdims=True)
            outs.append(o)
        attm_ref[:, sl] = jnp.where(lane < HEAD_DIM, outs[0], outs[1])


def _small_pre_call(xs, nmix, w_cat, bf_col, bf_row, cw, cb, wg, bga, bgx, lam, nrnn, hs0, cs0, cs1, cs2):
    ns = SMALL_ROWS - META_PAD
    shp = jax.ShapeDtypeStruct
    return pl.pallas_call(
        _small_pre_kernel,
        out_shape=[shp((ns, D_ATT), f32), shp((ns, D_ATT), f32), shp((ns, D_ATT), f32), shp((ns, N_HEADS), f32),
                   shp((ns, D_RNN), f32), shp((ns, D_RNN), f32),
                   shp((D_ATT, META_PAD), f32), shp((D_ATT, META_PAD), f32),
                   shp((META_PAD, D_ATT), f32), shp((META_PAD, D_ATT), f32),
                   shp((N_HEADS, META_PAD), f32), shp((N_HEADS, META_PAD), f32), shp((BIAS_ROWS, META_PAD), bf16),
                   shp((META_PAD, D_ATT), f32), shp((SUBLANES, D_RNN), f32), shp((SUBLANES, D_RNN), f32),
                   shp((SMALL_ROWS, D_RNN), f32)],
        compiler_params=pltpu.CompilerParams(vmem_limit_bytes=VMEM_LIMIT),
        name="small_pre",
    )(xs, nmix, w_cat, bf_col, bf_row, cw, cb, wg, bga, bgx, lam, nrnn, hs0, cs0, cs1, cs2)


def _small_post_kernel(x_ref, att_ref, rn_ref, natt_ref, woa_ref, wor_ref, nffn_ref, wup_ref, fw_ref, fb_ref,
                       wdn_ref, nfin_ref, f0_ref, f1_ref, ys_ref, gps_ref, gpm_ref):
    m0 = META_PAD
    an = _rms(att_ref[...], natt_ref[...]).astype(bf16)
    x1 = x_ref[...] + _dot(an, woa_ref[...]) + _dot(rn_ref[...].astype(bf16), wor_ref[...])
    hn = _rms(x1, nffn_ref[...]).astype(bf16)
    up = _dot(hn, wup_ref[...])
    gp = up[:, 0:D_FF]
    val = up[:, D_FF:]
    fw = fw_ref[...]
    gpm = gp[0:m0]
    gm = fb_ref[...] + fw[0:1] * _shift_rows(gpm, 2, 0.0) + fw[1:2] * _shift_rows(gpm, 1, 0.0) + fw[2:3] * gpm
    gps = gp[m0:]
    gs = fb_ref[...] + fw[0:1] * f0_ref[...] + fw[1:2] * f1_ref[...] + fw[2:3] * gps
    gps_ref[...] = gps
    gpm_ref[...] = gpm[SUBLANES:2 * SUBLANES, :]
    g = jnp.concatenate([gm, gs], axis=0)
    act = (_gelu(g) * val).astype(bf16)
    x2 = x1 + _dot(act, wdn_ref[...])
    ys_ref[...] = _rms(x2[m0:], nfin_ref[...])


def _small_post_call(xs, att, rn, natt, woa, wor, nffn, wup, fw, fb, wdn, nfin, f0, f1):
    ns = SMALL_ROWS - META_PAD
    shp = jax.ShapeDtypeStruct
    return pl.pallas_call(
        _small_post_kernel,
        out_shape=[shp((ns, D_MODEL), f32), shp((ns, D_FF), f32), shp((SUBLANES, D_FF), f32)],
        compiler_params=pltpu.CompilerParams(vmem_limit_bytes=VMEM_LIMIT),
        name="small_post",
    )(xs, att, rn, natt, woa, wor, nffn, wup, fw, fb, wdn, nfin, f0, f1)


def _paged_kernel(pt_ref, qbc_ref, q_ref, kn_ref, vn_ref, lfn_ref, ck_ref, cv_ref, cl_ref, o_ref,
                  buf, lbuf, s_scr, acc_scr, sem, lsem, *, n_seq, n_pages):
    b = pl.program_id(0)
    n_chunks = n_pages // PAGES_PER_CHUNK
    per_seq = 2 * n_chunks
    assert per_seq % N_SLOTS == 0 and n_pages % PAGES_PER_CHUNK == 0

    def chunk_copies(seq, c, slot):
        src = ck_ref if c < n_chunks else cv_ref
        p0 = (c % n_chunks) * PAGES_PER_CHUNK
        return [pltpu.make_async_copy(src.at[pt_ref[seq, p0 + pg]], buf.at[slot, pg], sem.at[slot])
                for pg in range(PAGES_PER_CHUNK)]

    def logf_copies(seq, slot):
        return [pltpu.make_async_copy(cl_ref.at[pt_ref[seq, pg]], lbuf.at[slot, pg], lsem.at[slot])
                for pg in range(n_pages)]

    lslot = b % 2
    nxt = jnp.minimum(b + 1, n_seq - 1)

    @pl.when(b == 0)
    def _():
        for cp in logf_copies(0, 0):
            cp.start()
        for c in range(N_SLOTS - 1):
            for cp in chunk_copies(0, c, c):
                cp.start()

    @pl.when(b + 1 < n_seq)
    def _():
        for cp in logf_copies(nxt, 1 - lslot):
            cp.start()

    acc_scr[...] = jnp.zeros(acc_scr.shape, f32)

    for c in range(per_seq):
        ahead = c + N_SLOTS - 1
        if ahead < per_seq:
            for cp in chunk_copies(b, ahead, ahead % N_SLOTS):
                cp.start()
        else:
            @pl.when(b + 1 < n_seq)
            def _():
                for cp in chunk_copies(nxt, ahead - per_seq, ahead % N_SLOTS):
                    cp.start()
        slot = c % N_SLOTS
        for cp in chunk_copies(b, c, slot):
            cp.wait()

        if c < n_chunks:
            for h in range(N_HEADS):
                qh = qbc_ref[0, h]
                for pg in range(PAGES_PER_CHUNK):
                    red = jnp.sum(buf[slot, pg, h] * qh, axis=0, keepdims=True)
                    s_scr[c * PAGES_PER_CHUNK + pg, h:h + 1, :] = red
        else:
            for h in range(N_HEADS):
                a = jnp.zeros((HEAD_DIM, PAGE), f32)
                for pg in range(PAGES_PER_CHUNK):
                    page = (c - n_chunks) * PAGES_PER_CHUNK + pg
                    a = a + buf[slot, pg, h] * s_scr[page, h:h + 1, :]
                acc_scr[h] += a

        if c == n_chunks - 1:
            for cp in logf_copies(b, lslot):
                cp.wait()
            rows = n_pages * N_HEADS
            lf = lbuf[lslot].reshape(rows, PAGE)
            cw = _cumsum_lanes(lf)
            tot = jnp.broadcast_to(cw[:, PAGE - 1:PAGE], (rows, PAGE))
            ridx = lax.broadcasted_iota(jnp.int32, (rows, PAGE), 0)
            off = tot
            sh = N_HEADS
            while sh < rows:
                off = off + jnp.where(ridx >= sh, pltpu.roll(off, sh, 0), 0.0)
                sh *= 2
            cfull = cw + (off - tot)
            s3 = (s_scr[...].reshape(rows, PAGE) - cfull).reshape(n_pages, N_HEADS, PAGE)
            c_past = off[rows - N_HEADS:rows, 0:1]
            s_new = jnp.sum(q_ref[0] * kn_ref[0], axis=1, keepdims=True) - (c_past + lfn_ref[0])
            m = jnp.max(jnp.max(s3, axis=0), axis=1, keepdims=True)
            m = jnp.maximum(m, s_new)
            p3 = jnp.exp(s3 - m[None])
            p_new = jnp.exp(s_new - m)
            l = jnp.sum(jnp.sum(p3, axis=0), axis=1, keepdims=True) + p_new
            s_scr[...] = p3

    num = jnp.sum(acc_scr[...], axis=2) + p_new * vn_ref[0]
    o_ref[0] = num / l


def _paged_call(page_table, qbc, q3, kn3, vn3, lfn3, ckT, cvT, clT):
    n_seq, n_pages = page_table.shape
    blk3 = lambda w: pl.BlockSpec((1, N_HEADS, w), lambda b, pt: (b, 0, 0))
    grid_spec = pltpu.PrefetchScalarGridSpec(
        num_scalar_prefetch=1,
        grid=(n_seq,),
        in_specs=[pl.BlockSpec((1, N_HEADS, HEAD_DIM, PAGE), lambda b, pt: (b, 0, 0, 0)),
                  blk3(HEAD_DIM), blk3(HEAD_DIM), blk3(HEAD_DIM), blk3(1),
                  pl.BlockSpec(memory_space=pl.ANY), pl.BlockSpec(memory_space=pl.ANY),
                  pl.BlockSpec(memory_space=pl.ANY)],
        out_specs=blk3(HEAD_DIM),
        scratch_shapes=[pltpu.VMEM((N_SLOTS, PAGES_PER_CHUNK, N_HEADS, HEAD_DIM, PAGE), f32),
                        pltpu.VMEM((2, n_pages, N_HEADS, PAGE), f32),
                        pltpu.VMEM((n_pages, N_HEADS, PAGE), f32),
                        pltpu.VMEM((N_HEADS, HEAD_DIM, PAGE), f32),
                        pltpu.SemaphoreType.DMA((N_SLOTS,)), pltpu.SemaphoreType.DMA((2,))])
    return pl.pallas_call(
        functools.partial(_paged_kernel, n_seq=n_seq, n_pages=n_pages),
        grid_spec=grid_spec,
        out_shape=jax.ShapeDtypeStruct((n_seq, N_HEADS, HEAD_DIM), f32),
        compiler_params=pltpu.CompilerParams(dimension_semantics=("arbitrary",), vmem_limit_bytes=VMEM_LIMIT),
        name="paged_attention",
    )(page_table, qbc, q3, kn3, vn3, lfn3, ckT, cvT, clT)


def _block_diag(w):
    n, r, c = w.shape
    eye = jnp.eye(n, dtype=w.dtype)
    return (eye[:, None, :, None] * w[:, :, None, :]).reshape(n * r, n * c)


def kernel(x_prompt, x_sample, cache_k, cache_v, cache_logf, state_rnn_h, state_rnn_conv, state_ffn_conv, page_table, meta_tokens, norm_mix, w_in, b_forget, rnn_conv_w, rnn_conv_b, w_rg_a, b_rg_a, w_rg_x, b_rg_x, rg_lambda, norm_att_out, norm_rnn_out, w_out, norm_ffn, w_ffn_up, ffn_conv_w, ffn_conv_b, w_ffn_down, norm_final):
    nb, seq, _ = x_prompt.shape
    db = x_sample.shape[0]
    assert norm_mix.shape[0] == 1 and x_sample.shape[1] == 1 and seq % TM == 0 and seq % TQ == 0

    w = w_in[0]
    wf = jnp.pad(w[:, 3 * D_ATT:3 * D_ATT + N_HEADS], ((0, 0), (0, LANES - N_HEADS)))
    w_cat = jnp.concatenate([w[:, :D_ATT] * ATT_SCALE, w[:, D_ATT:3 * D_ATT], w[:, 3 * D_ATT + N_HEADS:], wf],
                            axis=1).astype(bf16)
    bda = _block_diag(w_rg_a[0])
    bdx = _block_diag(w_rg_x[0])
    wg = jnp.stack([jnp.concatenate([bda[:256, :256], bdx[:256, :256]], axis=1),
                    jnp.concatenate([bda[256:, 256:], bdx[256:, 256:]], axis=1)]).astype(bf16)
    woa = w_out[0, :D_ATT].astype(bf16)
    wor = w_out[0, D_ATT:].astype(bf16)
    wup = w_ffn_up[0].astype(bf16)
    wdn = w_ffn_down[0].astype(bf16)
    bf_col = b_forget[0][:, None]
    bf_row = b_forget
    nfin = norm_final[None, :]
    mix_args = (norm_mix, w_cat, bf_col)
    rnn_args = (rnn_conv_w[0], rnn_conv_b, wg, b_rg_a, b_rg_x, rg_lambda, norm_rnn_out)
    post_args = (norm_att_out, woa, wor, norm_ffn, wup, ffn_conv_w[0], ffn_conv_b, wdn, nfin)

    xs = jnp.concatenate([meta_tokens, jnp.zeros((META_PAD - N_META, D_MODEL), f32), x_sample[:, 0, :]], axis=0)
    (q_s, k_s, v_s, lf_s, h_s, xr_s, kTm, vTm, km, vm, lfTm, cTm, caug_m, att_m, h_m, xr_m, rn_small) = _small_pre_call(
        xs, *mix_args, bf_row, *rnn_args, state_rnn_h[0],
        state_rnn_conv[0, :, 0], state_rnn_conv[0, :, 1], state_rnn_conv[0, :, 2])

    ckT = jnp.transpose(cache_k[0], (0, 2, 3, 1))
    cvT = jnp.transpose(cache_v[0], (0, 2, 3, 1))
    clT = jnp.transpose(cache_logf[0], (0, 2, 1))
    q3 = q_s.reshape(db, N_HEADS, HEAD_DIM)
    qbc = jnp.broadcast_to(q3[:, :, :, None], (db, N_HEADS, HEAD_DIM, PAGE))
    att_s = _paged_call(page_table, qbc, q3, k_s.reshape(db, N_HEADS, HEAD_DIM),
                        v_s.reshape(db, N_HEADS, HEAD_DIM), lf_s[:, :, None], ckT, cvT, clT)

    att_small = jnp.concatenate([att_m, att_s.reshape(db, D_ATT)], axis=0)
    y_s, gp_s, gp_m = _small_post_call(xs, att_small, rn_small, *post_args,
                                       state_ffn_conv[0, :, 0], state_ffn_conv[0, :, 1])

    (q, kT, vT, kTb, vb, lfT, caug, rn, h_last, xr_last) = _inproj_call(
        x_prompt, *mix_args, *rnn_args, h_m, xr_m, cTm)
    att = _attn_call(q, kTb, vb, caug, kTm.astype(bf16), vm.astype(bf16), caug_m)
    y_prompt, gp_last = _ffn_call(x_prompt, att, rn, *post_args, gp_m)

    def with_meta(meta_t, real_t):
        c = meta_t.shape[0]
        return jnp.concatenate([jnp.broadcast_to(meta_t[None, :, :N_META], (nb, c, N_META)), real_t], axis=2)

    t_all = N_META + seq
    k_prompt = jnp.transpose(with_meta(kTm, kT).reshape(nb, N_HEADS, HEAD_DIM, t_all), (0, 3, 1, 2))[None]
    v_prompt = jnp.transpose(with_meta(vTm, vT).reshape(nb, N_HEADS, HEAD_DIM, t_all), (0, 3, 1, 2))[None]
    logf_prompt = jnp.transpose(with_meta(lfTm, lfT), (0, 2, 1))[None]
    rnn_h_prompt = h_last[:, 0, :][None]
    rnn_conv_prompt = xr_last[:, SUBLANES - 3:, :][None]
    ffn_conv_prompt = gp_last[:, SUBLANES - 2:, :][None]

    k_sample = k_s.reshape(1, db, 1, N_HEADS, HEAD_DIM)
    v_sample = v_s.reshape(1, db, 1, N_HEADS, HEAD_DIM)
    logf_sample = lf_s.reshape(1, db, 1, N_HEADS)
    rnn_h_sample = h_s[None]
    rnn_conv_sample = jnp.stack([state_rnn_conv[0, :, 1], state_rnn_conv[0, :, 2], xr_s], axis=1)[None]
    ffn_conv_sample = jnp.stack([state_ffn_conv[0, :, 1], gp_s], axis=1)[None]
    y_sample = y_s[:, None, :]

    return (y_prompt, y_sample, k_prompt, v_prompt, logf_prompt, rnn_h_prompt, rnn_conv_prompt, ffn_conv_prompt,
            k_sample, v_sample, logf_sample, rnn_h_sample, rnn_conv_sample, ffn_conv_sample)
```

```python
import functools

import jax
import jax.numpy as jnp
from jax import lax
from jax.experimental import pallas as pl
from jax.experimental.pallas import tpu as pltpu

D_MODEL = 1024
N_META = 16
D_ATT = 512
D_RNN = 512
HEAD_DIM = 64
N_HEADS = 8
N_RNN_BLOCKS = 8
RG_C = 8.0
D_FF = 2816
EPS = 1e-6
ATT_SCALE = HEAD_DIM ** -0.5
PAGE = 128

LANES = 128
SUBLANES = 8
ZF_COL = 5 * 512
W_IN_COLS = ZF_COL + LANES
META_PAD = 128
SMALL_ROWS = META_PAD + 32
VMEM_LIMIT = 60 * 1024 * 1024

TM = 512
TQ = 512
TK = 512
ROW_STRIP = 32
ATTN_TRIP = 4
ATTN_COLS = ATTN_TRIP * TK + META_PAD
BIAS_ROWS = 32
LOG2E = 1.4426950408889634
FF_CHUNKS = ((0, 1536), (1536, 1280))
PAGES_PER_CHUNK = 8
N_SLOTS = 8

bf16 = jnp.bfloat16
f32 = jnp.float32


def _rms(x, g):
    return x * lax.rsqrt(jnp.mean(x * x, axis=-1, keepdims=True) + EPS) * g


def _gelu(x):
    return x * (0.5 * (1.0 + jnp.tanh(0.7978845608028654 * (x + 0.044715 * (x * x * x)))))


def _sigmoid(x):
    return 0.5 * (1.0 + jnp.tanh(0.5 * x))


def _softplus(x):
    return jnp.maximum(x, 0.0) + jnp.log1p(jnp.exp(-jnp.abs(x)))


def _log_sigmoid(x):
    return -_softplus(-x)


def _dot(a, b):
    return jnp.dot(a, b, preferred_element_type=f32)


def _shift_rows(x, s, fill):
    row = lax.broadcasted_iota(jnp.int32, x.shape, 0)
    return jnp.where(row >= s, pltpu.roll(x, s, 0), fill)


def _cumsum_lanes(x):
    n = x.shape[-1]
    lane = lax.broadcasted_iota(jnp.int32, x.shape, x.ndim - 1)
    s = 1
    while s < n:
        x = x + jnp.where(lane >= s, pltpu.roll(x, s, x.ndim - 1), 0.0)
        s *= 2
    return x


def _scan_rows(a, u):
    n = a.shape[0]
    s = 1
    while s < n:
        a_sh = _shift_rows(a, s, 1.0)
        u_sh = _shift_rows(u, s, 0.0)
        u = a * u_sh + u
        a = a * a_sh
        s *= 2
    return a, u


def _gates(xc, wg_ref, bga, bgx, lam):
    xcb = xc.astype(bf16)
    half = D_RNN // 2
    g0 = _dot(xcb[:, :half], wg_ref[0])
    g1 = _dot(xcb[:, half:], wg_ref[1])
    ga = jnp.concatenate([g0[:, :half], g1[:, :half]], axis=1) + bga
    gx = jnp.concatenate([g0[:, half:], g1[:, half:]], axis=1) + bgx
    r = _sigmoid(ga)
    ig = _sigmoid(gx)
    log_a = (-RG_C) * r * _softplus(-lam)
    a = jnp.exp(log_a)
    u = jnp.sqrt(-jnp.tanh(log_a) * (1.0 + a * a)) * (ig * xc)
    return a, u


def _inproj_kernel(x_ref, nmix_ref, w_ref, bfc_ref, cw_ref, cb_ref, wg_ref, bga_ref, bgx_ref, lam_ref,
                   nrnn_ref, h0_ref, xr0_ref, c0_ref,
                   q_ref, kT_ref, vT_ref, kTb_ref, vb_ref, lfT_ref, ca_ref, rn_ref, hlast_ref, xrlast_ref,
                   xp_scr, hc_scr, cc_scr):
    i = pl.program_id(1)

    @pl.when(i == 0)
    def _():
        xp_scr[0:SUBLANES, :] = xr0_ref[...]
        hc_scr[...] = h0_ref[SUBLANES - 1:SUBLANES, :]
        cc_scr[...] = c0_ref[:, N_META - 1:N_META]

    hn = _rms(x_ref[0], nmix_ref[...]).astype(bf16)
    zr = _dot(hn, w_ref[:, 1536:2560])
    xr = zr[:, 0:512]
    gr = zr[:, 512:1024]
    zqk = _dot(hn, w_ref[:, 0:1024])

    xp_scr[SUBLANES:SUBLANES + TM, :] = xr
    cw = cw_ref[...]
    xc = (cb_ref[...] + cw[0:1] * xp_scr[5:5 + TM, :] + cw[1:2] * xp_scr[6:6 + TM, :]
          + cw[2:3] * xp_scr[7:7 + TM, :] + cw[3:4] * xr)
    xp_scr[0:SUBLANES, :] = xr[TM - SUBLANES:TM, :]
    xrlast_ref[0] = xr[TM - SUBLANES:TM, :]

    a, u = _gates(xc, wg_ref, bga_ref[...], bgx_ref[...], lam_ref[...])
    zv = _dot(hn, w_ref[:, 1024:1536])
    zf = _dot(hn, w_ref[:, ZF_COL:ZF_COL + LANES])

    q_ref[0] = (zqk[:, 0:512] * LOG2E).astype(bf16)
    kT = zqk[:, 512:1024].T
    kT_ref[0] = kT
    kTb_ref[0] = kT.astype(bf16)

    hl = hc_scr[...]
    hs = []
    for r0 in range(0, TM, SUBLANES):
        ag, ug = _scan_rows(a[r0:r0 + SUBLANES], u[r0:r0 + SUBLANES])
        hg = ag * hl + ug
        hs.append(hg)
        hl = hg[SUBLANES - 1:SUBLANES, :]
    hc_scr[...] = hl
    hlast_ref[0] = hl

    vT_ref[0] = zv.T
    vb_ref[0] = zv.astype(bf16)
    lfT = _log_sigmoid(zf.T[0:N_HEADS, :] + bfc_ref[...])
    lfT_ref[0] = lfT
    cum = _cumsum_lanes(lfT) + cc_scr[...]
    ca_ref[0] = _bias_parts(cum)
    cc_scr[...] = cum[:, TM - 1:TM]

    y = jnp.concatenate(hs, axis=0) * _gelu(gr)
    rn_ref[0] = _rms(y, nrnn_ref[...]).astype(bf16)


def _inproj_call(x_real, nmix, w_cat, bf_col, cw, cb, wg, bga, bgx, lam, nrnn, h0, xr0, c0):
    nb, t, _ = x_real.shape
    nt = t // TM
    const = lambda shape: pl.BlockSpec(shape, lambda b, i: (0,) * len(shape), pipeline_mode=pl.Buffered(1))
    row_blk = lambda w: pl.BlockSpec((1, TM, w), lambda b, i: (b, i, 0))
    col_blk = lambda r: pl.BlockSpec((1, r, TM), lambda b, i: (b, 0, i))
    per_b = lambda r, w: pl.BlockSpec((1, r, w), lambda b, i: (b, 0, 0))
    return pl.pallas_call(
        _inproj_kernel,
        grid=(nb, nt),
        in_specs=[row_blk(D_MODEL), const((1, D_MODEL)), const((D_MODEL, W_IN_COLS)), const((N_HEADS, 1)),
                  const((4, D_RNN)), const((1, D_RNN)), const((2, 256, 512)), const((1, D_RNN)),
                  const((1, D_RNN)), const((1, D_RNN)), const((1, D_RNN)),
                  const((SUBLANES, D_RNN)), const((SUBLANES, D_RNN)), const((N_HEADS, LANES))],
        out_specs=[row_blk(D_ATT), col_blk(D_ATT), col_blk(D_ATT), col_blk(D_ATT), row_blk(D_ATT),
                   col_blk(N_HEADS), col_blk(BIAS_ROWS), row_blk(D_RNN), per_b(1, D_RNN), per_b(SUBLANES, D_RNN)],
        out_shape=[jax.ShapeDtypeStruct((nb, t, D_ATT), bf16),
                   jax.ShapeDtypeStruct((nb, D_ATT, t), f32),
                   jax.ShapeDtypeStruct((nb, D_ATT, t), f32),
                   jax.ShapeDtypeStruct((nb, D_ATT, t), bf16),
                   jax.ShapeDtypeStruct((nb, t, D_ATT), bf16),
                   jax.ShapeDtypeStruct((nb, N_HEADS, t), f32),
                   jax.ShapeDtypeStruct((nb, BIAS_ROWS, t), bf16),
                   jax.ShapeDtypeStruct((nb, t, D_RNN), bf16),
                   jax.ShapeDtypeStruct((nb, 1, D_RNN), f32),
                   jax.ShapeDtypeStruct((nb, SUBLANES, D_RNN), f32)],
        scratch_shapes=[pltpu.VMEM((TM + SUBLANES, D_RNN), f32),
                        pltpu.VMEM((1, D_RNN), f32), pltpu.VMEM((N_HEADS, 1), f32)],
        compiler_params=pltpu.CompilerParams(dimension_semantics=("arbitrary", "arbitrary"),
                                             vmem_limit_bytes=VMEM_LIMIT),
        name="inproj_rglru",
    )(x_real, nmix, w_cat, bf_col, cw, cb, wg, bga, bgx, lam, nrnn, h0, xr0, c0)


def _bias_parts(c):
    c2 = c * (-LOG2E)
    hi = c2.astype(bf16).astype(f32)
    r1 = c2 - hi
    mid = r1.astype(bf16).astype(f32)
    lo = (r1 - mid).astype(bf16).astype(f32)
    return jnp.concatenate([hi, mid, lo, jnp.zeros_like(hi)], axis=0).astype(bf16)


def _unit_softmax(j, c0, width, visible, s_scr, p_scr, m_scr, l_scr, acc_scr):
    for r0 in range(0, TQ, ROW_STRIP):
        rows = slice(r0, r0 + ROW_STRIP)
        st = []
        for t in range(width // LANES):
            vis = visible(r0, t)
            if vis is False:
                st.append(None)
                continue
            x = s_scr[j, rows, c0 + t * LANES:c0 + (t + 1) * LANES]
            st.append(x if vis is True else jnp.where(vis, x, -jnp.inf))
        live = [x for x in st if x is not None]
        mx = live[0]
        for x in live[1:]:
            mx = jnp.maximum(mx, x)
        m_old = m_scr[j, rows, :]
        m_new = jnp.maximum(m_old, jnp.max(mx, axis=1, keepdims=True))
        alpha = jnp.exp2(m_old - m_new)
        psum = None
        for t, x in enumerate(st):
            cols = slice(c0 + t * LANES, c0 + (t + 1) * LANES)
            if x is None:
                p_scr[j, rows, cols] = jnp.zeros((ROW_STRIP, LANES), bf16)
                continue
            p = jnp.exp2(x - m_new)
            psum = p if psum is None else psum + p
            p_scr[j, rows, cols] = p.astype(bf16)
        m_scr[j, rows, :] = m_new
        l_scr[j, rows, :] = alpha * l_scr[j, rows, :] + psum
        acc_scr[j, rows, :] = alpha * acc_scr[j, rows, :]


def _attn_units(units, qas, s_scr, p_scr, m_scr, l_scr, acc_scr):
    def scores(u):
        j, c0, width, kTa, _, _ = u
        s_scr[j, :, c0:c0 + width] = _dot(qas[j], kTa)

    def values(u):
        j, c0, width, _, v, _ = u
        acc_scr[j] += _dot(p_scr[j, :, c0:c0 + width], v)

    for u in units[:2]:
        scores(u)
    for n, u in enumerate(units):
        j, c0, width, _, _, visible = u
        _unit_softmax(j, c0, width, visible, s_scr, p_scr, m_scr, l_scr, acc_scr)
        if n + 2 < len(units):
            scores(units[n + 2])
        values(u)


def _attn_kernel(q_ref, kT_ref, v_ref, ca_ref, kTm_ref, vm_ref, cam_ref, o_ref, s_scr, p_scr, m_scr, l_scr, acc_scr):
    hp = pl.program_id(1)
    i = pl.program_id(2)
    q = q_ref[0].astype(f32)
    lane = lax.broadcasted_iota(jnp.int32, (TQ, LANES), 1)
    qas = []
    for j in range(2):
        h = 2 * hp + j
        qj = jnp.where((lane >= j * HEAD_DIM) & (lane < (j + 1) * HEAD_DIM), q, 0.0)
        pick = jnp.where((lane == h) | (lane == h + N_HEADS) | (lane == h + 2 * N_HEADS), 1.0, 0.0)
        qas.append(jnp.concatenate([qj, pick], axis=1).astype(bf16))
    scr = (s_scr, p_scr, m_scr, l_scr, acc_scr)
    m_scr[...] = jnp.full(m_scr.shape, -jnp.inf, f32)
    l_scr[...] = jnp.zeros(l_scr.shape, f32)
    acc_scr[...] = jnp.zeros(acc_scr.shape, f32)

    def keys_aug(k0):
        pad = jnp.zeros((LANES - BIAS_ROWS, TK), bf16)
        return jnp.concatenate([kT_ref[0, :, pl.ds(k0, TK)], ca_ref[0, :, pl.ds(k0, TK)], pad], axis=0)

    def tile_units(k0, n_tiles):
        units = []
        for a in range(n_tiles):
            ka = pl.multiple_of(k0 + a * TK, TK)
            kTa = keys_aug(ka)
            v = v_ref[0, pl.ds(ka, TK), :]
            units += [(j, a * TK, TK, kTa, v, lambda r0, t: True) for j in range(2)]
        return units

    def body(kt, carry):
        _attn_units(tile_units(kt * (ATTN_TRIP * TK), ATTN_TRIP), qas, *scr)
        return carry

    lax.fori_loop(0, i // ATTN_TRIP, body, 0)

    slane = lax.broadcasted_iota(jnp.int32, (ROW_STRIP, LANES), 1)
    srow = lax.broadcasted_iota(jnp.int32, (ROW_STRIP, LANES), 0)

    def last_visible(r0, t):
        if t == 0:
            return slane < N_META
        c0 = (t - 1) * LANES
        if c0 + LANES - 1 <= r0:
            return True
        if c0 > r0 + ROW_STRIP - 1:
            return False
        return slane + c0 <= srow + r0

    def last_units(col0):
        k0 = pl.multiple_of(i * TQ, TQ)
        meta_aug = jnp.concatenate([kTm_ref[...], cam_ref[...],
                                    jnp.zeros((LANES - BIAS_ROWS, META_PAD), bf16)], axis=0)
        kTa = jnp.concatenate([meta_aug, keys_aug(k0)], axis=1)
        v = jnp.concatenate([vm_ref[...], v_ref[0, pl.ds(k0, TK), :]], axis=0)
        return [(j, col0, META_PAD + TK, kTa, v, last_visible) for j in range(2)]

    for rem in range(ATTN_TRIP):
        @pl.when(i % ATTN_TRIP == rem)
        def _(rem=rem):
            _attn_units(tile_units((i - rem) * TK, rem) + last_units(rem * TK), qas, *scr)

    outs = [acc_scr[j] / jnp.sum(l_scr[j], axis=1, keepdims=True) for j in range(2)]
    o_ref[0] = jnp.where(lane < HEAD_DIM, outs[0], outs[1])


def _attn_call(q, kTb, vb, caug, kTm, vm, caugm):
    nb, t, _ = q.shape
    npair = N_HEADS // 2
    return pl.pallas_call(
        _attn_kernel,
        grid=(nb, npair, t // TQ),
        in_specs=[pl.BlockSpec((1, TQ, LANES), lambda b, h, i: (b, i, h)),
                  pl.BlockSpec((1, LANES, t), lambda b, h, i: (b, h, 0)),
                  pl.BlockSpec((1, t, LANES), lambda b, h, i: (b, 0, h)),
                  pl.BlockSpec((1, BIAS_ROWS, t), lambda b, h, i: (b, 0, 0)),
                  pl.BlockSpec((LANES, META_PAD), lambda b, h, i: (h, 0)),
                  pl.BlockSpec((META_PAD, LANES), lambda b, h, i: (0, h)),
                  pl.BlockSpec((BIAS_ROWS, META_PAD), lambda b, h, i: (0, 0))],
        out_specs=pl.BlockSpec((1, TQ, LANES), lambda b, h, i: (b, i, h)),
        out_shape=jax.ShapeDtypeStruct((nb, t, D_ATT), f32),
        scratch_shapes=[pltpu.VMEM((2, TQ, ATTN_COLS), f32), pltpu.VMEM((2, TQ, ATTN_COLS), bf16),
                        pltpu.VMEM((2, TQ, LANES), f32), pltpu.VMEM((2, TQ, LANES), f32),
                        pltpu.VMEM((2, TQ, LANES), f32)],
        compiler_params=pltpu.CompilerParams(dimension_semantics=("arbitrary", "arbitrary", "arbitrary"),
                                             vmem_limit_bytes=VMEM_LIMIT),
        name="prompt_attention",
    )(q, kTb, vb, caug, kTm, vm, caugm)


def _ffn_kernel(x_ref, att_ref, rn_ref, natt_ref, woa_ref, wor_ref, nffn_ref, wup_ref, fw_ref, fb_ref,
                wdn_ref, nfin_ref, gp0_ref, y_ref, gplast_ref, gp_scr):
    i = pl.program_id(1)

    @pl.when(i == 0)
    def _():
        gp_scr[0:SUBLANES, :] = gp0_ref[...]

    an = _rms(att_ref[0], natt_ref[...]).astype(bf16)
    x1 = x_ref[0] + _dot(an, woa_ref[...]) + _dot(rn_ref[0], wor_ref[...])
    hn = _rms(x1, nffn_ref[...]).astype(bf16)
    down = None
    for c0, wd in FF_CHUNKS:
        cols = slice(c0, c0 + wd)
        gp = _dot(hn, wup_ref[:, cols])
        val = _dot(hn, wup_ref[:, D_FF + c0:D_FF + c0 + wd])
        gp_scr[SUBLANES:SUBLANES + TM, cols] = gp
        fw = fw_ref[:, cols]
        g = (fb_ref[:, cols] + fw[0:1] * gp_scr[6:6 + TM, cols] + fw[1:2] * gp_scr[7:7 + TM, cols]
             + fw[2:3] * gp)
        act = (_gelu(g) * val).astype(bf16)
        part = _dot(act, wdn_ref[cols, :])
        down = part if down is None else down + part
    last = gp_scr[TM:TM + SUBLANES, :]
    gp_scr[0:SUBLANES, :] = last
    gplast_ref[0] = last
    y_ref[0] = _rms(x1 + down, nfin_ref[...])


def _ffn_call(x_real, att, rn, natt, woa, wor, nffn, wup, fw, fb, wdn, nfin, gp0):
    nb, t, _ = x_real.shape
    const = lambda shape: pl.BlockSpec(shape, lambda b, i: (0,) * len(shape), pipeline_mode=pl.Buffered(1))
    row_blk = lambda w: pl.BlockSpec((1, TM, w), lambda b, i: (b, i, 0))
    return pl.pallas_call(
        _ffn_kernel,
        grid=(nb, t // TM),
        in_specs=[row_blk(D_MODEL), row_blk(D_ATT), row_blk(D_RNN), const((1, D_ATT)),
                  const((D_ATT, D_MODEL)), const((D_RNN, D_MODEL)), const((1, D_MODEL)),
                  const((D_MODEL, 2 * D_FF)), const((3, D_FF)), const((1, D_FF)),
                  const((D_FF, D_MODEL)), const((1, D_MODEL)), const((SUBLANES, D_FF))],
        out_specs=[row_blk(D_MODEL), pl.BlockSpec((1, SUBLANES, D_FF), lambda b, i: (b, 0, 0))],
        out_shape=[jax.ShapeDtypeStruct((nb, t, D_MODEL), f32),
                   jax.ShapeDtypeStruct((nb, SUBLANES, D_FF), f32)],
        scratch_shapes=[pltpu.VMEM((TM + SUBLANES, D_FF), f32)],
        compiler_params=pltpu.CompilerParams(dimension_semantics=("arbitrary", "arbitrary"),
                                             vmem_limit_bytes=VMEM_LIMIT),
        name="outproj_ffn",
    )(x_real, att, rn, natt, woa, wor, nffn, wup, fw, fb, wdn, nfin, gp0)


def _small_pre_kernel(x_ref, nmix_ref, w_ref, bfc_ref, bfr_ref, cw_ref, cb_ref, wg_ref, bga_ref, bgx_ref,
                      lam_ref, nrnn_ref, hs0_ref, cs0_ref, cs1_ref, cs2_ref,
                      qs_ref, ks_ref, vs_ref, lfs_ref, hs_ref, xrs_ref,
                      kTm_ref, vTm_ref, km_ref, vm_ref, lfTm_ref, cTm_ref, cam_ref, attm_ref, hm_ref, xrm_ref, rn_ref):
    m0 = META_PAD
    hn = _rms(x_ref[...], nmix_ref[...]).astype(bf16)
    z = _dot(hn, w_ref[...])
    cw = cw_ref[...]
    lam = lam_ref[...]

    zs = z[m0:, :]
    qs_ref[...] = zs[:, 0:512]
    ks_ref[...] = zs[:, 512:1024]
    vs_ref[...] = zs[:, 1024:1536]
    lfs_ref[...] = _log_sigmoid(zs[:, ZF_COL:ZF_COL + N_HEADS] + bfr_ref[...])
    xrs = zs[:, 1536:2048]
    xrs_ref[...] = xrs
    xcs = cb_ref[...] + cw[0:1] * cs0_ref[...] + cw[1:2] * cs1_ref[...] + cw[2:3] * cs2_ref[...] + cw[3:4] * xrs
    a_s, u_s = _gates(xcs, wg_ref, bga_ref[...], bgx_ref[...], lam)
    h_s = a_s * hs0_ref[...] + u_s
    hs_ref[...] = h_s
    rn_ref[m0:, :] = _rms(h_s * _gelu(zs[:, 2048:2560]), nrnn_ref[...])

    zm = z[0:m0, :]
    km = zm[:, 512:1024]
    vm = zm[:, 1024:1536]
    kTm = km.T
    kTm_ref[...] = kTm
    vTm_ref[...] = vm.T
    km_ref[...] = km
    vm_ref[...] = vm
    lfT = _log_sigmoid(zm[:, ZF_COL:ZF_COL + LANES].T[0:N_HEADS, :] + bfc_ref[...])
    lfTm_ref[...] = lfT
    cT = _cumsum_lanes(lfT)
    cTm_ref[...] = cT
    cam_ref[...] = _bias_parts(cT)

    xrm = zm[:, 1536:2048]
    xrm_ref[...] = xrm[SUBLANES:2 * SUBLANES, :]
    xcm = (cb_ref[...] + cw[0:1] * _shift_rows(xrm, 3, 0.0) + cw[1:2] * _shift_rows(xrm, 2, 0.0)
           + cw[2:3] * _shift_rows(xrm, 1, 0.0) + cw[3:4] * xrm)
    a_m, u_m = _gates(xcm[0:N_META], wg_ref, bga_ref[...], bgx_ref[...], lam)
    _, h_m = _scan_rows(a_m, u_m)
    hm_ref[...] = h_m[SUBLANES:2 * SUBLANES, :]
    rn_ref[0:N_META, :] = _rms(h_m * _gelu(zm[0:N_META, 2048:2560]), nrnn_ref[...])
    rn_ref[N_META:m0, :] = jnp.zeros((m0 - N_META, D_RNN), f32)

    qm = zm[:, 0:512].astype(bf16)
    kTb = kTm.astype(bf16)
    vb = vm.astype(bf16)
    lane = lax.broadcasted_iota(jnp.int32, (m0, LANES), 1)
    row = lax.broadcasted_iota(jnp.int32, (m0, LANES), 0)
    mask = (lane <= row) & (lane < N_META)
    for hp in range(N_HEADS // 2):
        sl = slice(hp * LANES, (hp + 1) * LANES)
        qp = qm[:, sl]
        outs = []
        for j in range(2):
            h = 2 * hp + j
            qj = jnp.where((lane >= j * HEAD_DIM) & (lane < (j + 1) * HEAD_DIM), qp, jnp.zeros_like(qp))
            s = _dot(qj, kTb[sl, :]) - cT[h:h + 1, :]
            s = jnp.where(mask, s, -jnp.inf)
            p = jnp.exp(s - jnp.max(s, axis=1, keepdims=True))
            o = _dot(p.astype(bf16), vb[:, sl]) / jnp.sum(p, axis=1, keepdims=True)
            outs.append(o)
        attm_ref[:, sl] = jnp.where(lane < HEAD_DIM, outs[0], outs[1])


def _small_pre_call(xs, nmix, w_cat, bf_col, bf_row, cw, cb, wg, bga, bgx, lam, nrnn, hs0, cs0, cs1, cs2):
    ns = SMALL_ROWS - META_PAD
    shp = jax.ShapeDtypeStruct
    return pl.pallas_call(
        _small_pre_kernel,
        out_shape=[shp((ns, D_ATT), f32), shp((ns, D_ATT), f32), shp((ns, D_ATT), f32), shp((ns, N_HEADS), f32),
                   shp((ns, D_RNN), f32), shp((ns, D_RNN), f32),
                   shp((D_ATT, META_PAD), f32), shp((D_ATT, META_PAD), f32),
                   shp((META_PAD, D_ATT), f32), shp((META_PAD, D_ATT), f32),
                   shp((N_HEADS, META_PAD), f32), shp((N_HEADS, META_PAD), f32), shp((BIAS_ROWS, META_PAD), bf16),
                   shp((META_PAD, D_ATT), f32), shp((SUBLANES, D_RNN), f32), shp((SUBLANES, D_RNN), f32),
                   shp((SMALL_ROWS, D_RNN), f32)],
        compiler_params=pltpu.CompilerParams(vmem_limit_bytes=VMEM_LIMIT),
        name="small_pre",
    )(xs, nmix, w_cat, bf_col, bf_row, cw, cb, wg, bga, bgx, lam, nrnn, hs0, cs0, cs1, cs2)


def _small_post_kernel(x_ref, att_ref, rn_ref, natt_ref, woa_ref, wor_ref, nffn_ref, wup_ref, fw_ref, fb_ref,
                       wdn_ref, nfin_ref, f0_ref, f1_ref, ys_ref, gps_ref, gpm_ref):
    m0 = META_PAD
    an = _rms(att_ref[...], natt_ref[...]).astype(bf16)
    x1 = x_ref[...] + _dot(an, woa_ref[...]) + _dot(rn_ref[...].astype(bf16), wor_ref[...])
    hn = _rms(x1, nffn_ref[...]).astype(bf16)
    up = _dot(hn, wup_ref[...])
    gp = up[:, 0:D_FF]
    val = up[:, D_FF:]
    fw = fw_ref[...]
    gpm = gp[0:m0]
    gm = fb_ref[...] + fw[0:1] * _shift_rows(gpm, 2, 0.0) + fw[1:2] * _shift_rows(gpm, 1, 0.0) + fw[2:3] * gpm
    gps = gp[m0:]
    gs = fb_ref[...] + fw[0:1] * f0_ref[...] + fw[1:2] * f1_ref[...] + fw[2:3] * gps
    gps_ref[...] = gps
    gpm_ref[...] = gpm[SUBLANES:2 * SUBLANES, :]
    g = jnp.concatenate([gm, gs], axis=0)
    act = (_gelu(g) * val).astype(bf16)
    x2 = x1 + _dot(act, wdn_ref[...])
    ys_ref[...] = _rms(x2[m0:], nfin_ref[...])


def _small_post_call(xs, att, rn, natt, woa, wor, nffn, wup, fw, fb, wdn, nfin, f0, f1):
    ns = SMALL_ROWS - META_PAD
    shp = jax.ShapeDtypeStruct
    return pl.pallas_call(
        _small_post_kernel,
        out_shape=[shp((ns, D_MODEL), f32), shp((ns, D_FF), f32), shp((SUBLANES, D_FF), f32)],
        compiler_params=pltpu.CompilerParams(vmem_limit_bytes=VMEM_LIMIT),
        name="small_post",
    )(xs, att, rn, natt, woa, wor, nffn, wup, fw, fb, wdn, nfin, f0, f1)


def _paged_kernel(pt_ref, qbc_ref, q_ref, kn_ref, vn_ref, lfn_ref, ck_ref, cv_ref, cl_ref, o_ref,
                  buf, lbuf, s_scr, acc_scr, sem, lsem, *, n_seq, n_pages):
    b = pl.program_id(0)
    n_chunks = n_pages // PAGES_PER_CHUNK
    per_seq = 2 * n_chunks
    assert per_seq % N_SLOTS == 0 and n_pages % PAGES_PER_CHUNK == 0

    def chunk_copies(seq, c, slot):
        src = ck_ref if c < n_chunks else cv_ref
        p0 = (c % n_chunks) * PAGES_PER_CHUNK
        return [pltpu.make_async_copy(src.at[pt_ref[seq, p0 + pg]], buf.at[slot, pg], sem.at[slot])
                for pg in range(PAGES_PER_CHUNK)]

    def logf_copies(seq, slot):
        return [pltpu.make_async_copy(cl_ref.at[pt_ref[seq, pg]], lbuf.at[slot, pg], lsem.at[slot])
                for pg in range(n_pages)]

    lslot = b % 2
    nxt = jnp.minimum(b + 1, n_seq - 1)

    @pl.when(b == 0)
    def _():
        for cp in logf_copies(0, 0):
            cp.start()
        for c in range(N_SLOTS - 1):
            for cp in chunk_copies(0, c, c):
                cp.start()

    @pl.when(b + 1 < n_seq)
    def _():
        for cp in logf_copies(nxt, 1 - lslot):
            cp.start()

    acc_scr[...] = jnp.zeros(acc_scr.shape, f32)

    for c in range(per_seq):
        ahead = c + N_SLOTS - 1
        if ahead < per_seq:
            for cp in chunk_copies(b, ahead, ahead % N_SLOTS):
                cp.start()
        else:
            @pl.when(b + 1 < n_seq)
            def _():
                for cp in chunk_copies(nxt, ahead - per_seq, ahead % N_SLOTS):
                    cp.start()
        slot = c % N_SLOTS
        for cp in chunk_copies(b, c, slot):
            cp.wait()

        if c < n_chunks:
            for h in range(N_HEADS):
                qh = qbc_ref[0, h]
                for pg in range(PAGES_PER_CHUNK):
                    red = jnp.sum(buf[slot, pg, h] * qh, axis=0, keepdims=True)
                    s_scr[c * PAGES_PER_CHUNK + pg, h:h + 1, :] = red
        else:
            for h in range(N_HEADS):
                a = jnp.zeros((HEAD_DIM, PAGE), f32)
                for pg in range(PAGES_PER_CHUNK):
                    page = (c - n_chunks) * PAGES_PER_CHUNK + pg
                    a = a + buf[slot, pg, h] * s_scr[page, h:h + 1, :]
                acc_scr[h] += a

        if c == n_chunks - 1:
            for cp in logf_copies(b, lslot):
                cp.wait()
            rows = n_pages * N_HEADS
            lf = lbuf[lslot].reshape(rows, PAGE)
            cw = _cumsum_lanes(lf)
            tot = jnp.broadcast_to(cw[:, PAGE - 1:PAGE], (rows, PAGE))
            ridx = lax.broadcasted_iota(jnp.int32, (rows, PAGE), 0)
            off = tot
            sh = N_HEADS
            while sh < rows:
                off = off + jnp.where(ridx >= sh, pltpu.roll(off, sh, 0), 0.0)
                sh *= 2
            cfull = cw + (off - tot)
            s3 = (s_scr[...].reshape(rows, PAGE) - cfull).reshape(n_pages, N_HEADS, PAGE)
            c_past = off[rows - N_HEADS:rows, 0:1]
            s_new = jnp.sum(q_ref[0] * kn_ref[0], axis=1, keepdims=True) - (c_past + lfn_ref[0])
            m = jnp.max(jnp.max(s3, axis=0), axis=1, keepdims=True)
            m = jnp.maximum(m, s_new)
            p3 = jnp.exp(s3 - m[None])
            p_new = jnp.exp(s_new - m)
            l = jnp.sum(jnp.sum(p3, axis=0), axis=1, keepdims=True) + p_new
            s_scr[...] = p3

    num = jnp.sum(acc_scr[...], axis=2) + p_new * vn_ref[0]
    o_ref[0] = num / l


def _paged_call(page_table, qbc, q3, kn3, vn3, lfn3, ckT, cvT, clT):
    n_seq, n_pages = page_table.shape
    blk3 = lambda w: pl.BlockSpec((1, N_HEADS, w), lambda b, pt: (b, 0, 0))
    grid_spec = pltpu.PrefetchScalarGridSpec(
        num_scalar_prefetch=1,
        grid=(n_seq,),
        in_specs=[pl.BlockSpec((1, N_HEADS, HEAD_DIM, PAGE), lambda b, pt: (b, 0, 0, 0)),
                  blk3(HEAD_DIM), blk3(HEAD_DIM), blk3(HEAD_DIM), blk3(1),
                  pl.BlockSpec(memory_space=pl.ANY), pl.BlockSpec(memory_space=pl.ANY),
                  pl.BlockSpec(memory_space=pl.ANY)],
        out_specs=blk3(HEAD_DIM),
        scratch_shapes=[pltpu.VMEM((N_SLOTS, PAGES_PER_CHUNK, N_HEADS, HEAD_DIM, PAGE), f32),
                        pltpu.VMEM((2, n_pages, N_HEADS, PAGE), f32),
                        pltpu.VMEM((n_pages, N_HEADS, PAGE), f32),
                        pltpu.VMEM((N_HEADS, HEAD_DIM, PAGE), f32),
                        pltpu.SemaphoreType.DMA((N_SLOTS,)), pltpu.SemaphoreType.DMA((2,))])
    return pl.pallas_call(
        functools.partial(_paged_kernel, n_seq=n_seq, n_pages=n_pages),
        grid_spec=grid_spec,
        out_shape=jax.ShapeDtypeStruct((n_seq, N_HEADS, HEAD_DIM), f32),
        compiler_params=pltpu.CompilerParams(dimension_semantics=("arbitrary",), vmem_limit_bytes=VMEM_LIMIT),
        name="paged_attention",
    )(page_table, qbc, q3, kn3, vn3, lfn3, ckT, cvT, clT)


def _block_diag(w):
    n, r, c = w.shape
    eye = jnp.eye(n, dtype=w.dtype)
    return (eye[:, None, :, None] * w[:, :, None, :]).reshape(n * r, n * c)


def kernel(x_prompt, x_sample, cache_k, cache_v, cache_logf, state_rnn_h, state_rnn_conv, state_ffn_conv, page_table, meta_tokens, norm_mix, w_in, b_forget, rnn_conv_w, rnn_conv_b, w_rg_a, b_rg_a, w_rg_x, b_rg_x, rg_lambda, norm_att_out, norm_rnn_out, w_out, norm_ffn, w_ffn_up, ffn_conv_w, ffn_conv_b, w_ffn_down, norm_final):
    nb, seq, _ = x_prompt.shape
    db = x_sample.shape[0]
    assert norm_mix.shape[0] == 1 and x_sample.shape[1] == 1 and seq % TM == 0 and seq % TQ == 0

    w = w_in[0]
    wf = jnp.pad(w[:, 3 * D_ATT:3 * D_ATT + N_HEADS], ((0, 0), (0, LANES - N_HEADS)))
    w_cat = jnp.concatenate([w[:, :D_ATT] * ATT_SCALE, w[:, D_ATT:3 * D_ATT], w[:, 3 * D_ATT + N_HEADS:], wf],
                            axis=1).astype(bf16)
    bda = _block_diag(w_rg_a[0])
    bdx = _block_diag(w_rg_x[0])
    wg = jnp.stack([jnp.concatenate([bda[:256, :256], bdx[:256, :256]], axis=1),
                    jnp.concatenate([bda[256:, 256:], bdx[256:, 256:]], axis=1)]).astype(bf16)
    woa = w_out[0, :D_ATT].astype(bf16)
    wor = w_out[0, D_ATT:].astype(bf16)
    wup = w_ffn_up[0].astype(bf16)
    wdn = w_ffn_down[0].astype(bf16)
    bf_col = b_forget[0][:, None]
    bf_row = b_forget
    nfin = norm_final[None, :]
    mix_args = (norm_mix, w_cat, bf_col)
    rnn_args = (rnn_conv_w[0], rnn_conv_b, wg, b_rg_a, b_rg_x, rg_lambda, norm_rnn_out)
    post_args = (norm_att_out, woa, wor, norm_ffn, wup, ffn_conv_w[0], ffn_conv_b, wdn, nfin)

    xs = jnp.concatenate([meta_tokens, jnp.zeros((META_PAD - N_META, D_MODEL), f32), x_sample[:, 0, :]], axis=0)
    (q_s, k_s, v_s, lf_s, h_s, xr_s, kTm, vTm, km, vm, lfTm, cTm, caug_m, att_m, h_m, xr_m, rn_small) = _small_pre_call(
        xs, *mix_args, bf_row, *rnn_args, state_rnn_h[0],
        state_rnn_conv[0, :, 0], state_rnn_conv[0, :, 1], state_rnn_conv[0, :, 2])

    ckT = jnp.transpose(cache_k[0], (0, 2, 3, 1))
    cvT = jnp.transpose(cache_v[0], (0, 2, 3, 1))
    clT = jnp.transpose(cache_logf[0], (0, 2, 1))
    q3 = q_s.reshape(db, N_HEADS, HEAD_DIM)
    qbc = jnp.broadcast_to(q3[:, :, :, None], (db, N_HEADS, HEAD_DIM, PAGE))
    att_s = _paged_call(page_table, qbc, q3, k_s.reshape(db, N_HEADS, HEAD_DIM),
                        v_s.reshape(db, N_HEADS, HEAD_DIM), lf_s[:, :, None], ckT, cvT, clT)

    att_small = jnp.concatenate([att_m, att_s.reshape(db, D_ATT)], axis=0)
    y_s, gp_s, gp_m = _small_post_call(xs, att_small, rn_small, *post_args,
                                       state_ffn_conv[0, :, 0], state_ffn_conv[0, :, 1])

    (q, kT, vT, kTb, vb, lfT, caug, rn, h_last, xr_last) = _inproj_call(
        x_prompt, *mix_args, *rnn_args, h_m, xr_m, cTm)
    att = _attn_call(q, kTb, vb, caug, kTm.astype(bf16), vm.astype(bf16), caug_m)
    y_prompt, gp_last = _ffn_call(x_prompt, att, rn, *post_args, gp_m)

    def with_meta(meta_t, real_t):
        c = meta_t.shape[0]
        return jnp.concatenate([jnp.broadcast_to(meta_t[None, :, :N_META], (nb, c, N_META)), real_t], axis=2)

    t_all = N_META + seq
    k_prompt = jnp.transpose(with_meta(kTm, kT).reshape(nb, N_HEADS, HEAD_DIM, t_all), (0, 3, 1, 2))[None]
    v_prompt = jnp.transpose(with_meta(vTm, vT).reshape(nb, N_HEADS, HEAD_DIM, t_all), (0, 3, 1, 2))[None]
    logf_prompt = jnp.transpose(with_meta(lfTm, lfT), (0, 2, 1))[None]
    rnn_h_prompt = h_last[:, 0, :][None]
    rnn_conv_prompt = xr_last[:, SUBLANES - 3:, :][None]
    ffn_conv_prompt = gp_last[:, SUBLANES - 2:, :][None]

    k_sample = k_s.reshape(1, db, 1, N_HEADS, HEAD_DIM)
    v_sample = v_s.reshape(1, db, 1, N_HEADS, HEAD_DIM)
    logf_sample = lf_s.reshape(1, db, 1, N_HEADS)
    rnn_h_sample = h_s[None]
    rnn_conv_sample = jnp.stack([state_rnn_conv[0, :, 1], state_rnn_conv[0, :, 2], xr_s], axis=1)[None]
    ffn_conv_sample = jnp.stack([state_ffn_conv[0, :, 1], gp_s], axis=1)[None]
    y_sample = y_s[:, None, :]

    return (y_prompt, y_sample, k_prompt, v_prompt, logf_prompt, rnn_h_prompt, rnn_conv_prompt, ffn_conv_prompt,
            k_sample, v_sample, logf_sample, rnn_h_sample, rnn_conv_sample, ffn_conv_sample)
```

```python
import functools

import jax
import jax.numpy as jnp
from jax import lax
from jax.experimental import pallas as pl
from jax.experimental.pallas import tpu as pltpu

D_MODEL = 1024
N_META = 16
D_ATT = 512
D_RNN = 512
HEAD_DIM = 64
N_HEADS = 8
N_RNN_BLOCKS = 8
RG_C = 8.0
D_FF = 2816
EPS = 1e-6
ATT_SCALE = HEAD_DIM ** -0.5
PAGE = 128

LANES = 128
SUBLANES = 8
ZF_COL = 5 * 512
W_IN_COLS = ZF_COL + LANES
META_PAD = 128
SMALL_ROWS = META_PAD + 32
VMEM_LIMIT = 60 * 1024 * 1024

TM = 512
TQ = 512
TK = 512
ROW_STRIP = 32
ATTN_TRIP = 4
ATTN_COLS = ATTN_TRIP * TK + META_PAD
BIAS_ROWS = 32
LOG2E = 1.4426950408889634
FF_CHUNKS = ((0, 1536), (1536, 1280))
PAGES_PER_CHUNK = 8
N_SLOTS = 8

bf16 = jnp.bfloat16
f32 = jnp.float32


def _rms(x, g):
    return x * lax.rsqrt(jnp.mean(x * x, axis=-1, keepdims=True) + EPS) * g


def _gelu(x):
    return x * (0.5 * (1.0 + jnp.tanh(0.7978845608028654 * (x + 0.044715 * (x * x * x)))))


def _sigmoid(x):
    return 0.5 * (1.0 + jnp.tanh(0.5 * x))


def _softplus(x):
    return jnp.maximum(x, 0.0) + jnp.log1p(jnp.exp(-jnp.abs(x)))


def _log_sigmoid(x):
    return -_softplus(-x)


def _dot(a, b):
    return jnp.dot(a, b, preferred_element_type=f32)


def _shift_rows(x, s, fill):
    row = lax.broadcasted_iota(jnp.int32, x.shape, 0)
    return jnp.where(row >= s, pltpu.roll(x, s, 0), fill)


def _cumsum_lanes(x):
    n = x.shape[-1]
    lane = lax.broadcasted_iota(jnp.int32, x.shape, x.ndim - 1)
    s = 1
    while s < n:
        x = x + jnp.where(lane >= s, pltpu.roll(x, s, x.ndim - 1), 0.0)
        s *= 2
    return x


def _scan_rows(a, u):
    n = a.shape[0]
    s = 1
    while s < n:
        a_sh = _shift_rows(a, s, 1.0)
        u_sh = _shift_rows(u, s, 0.0)
        u = a * u_sh + u
        a = a * a_sh
        s *= 2
    return a, u


def _gates(xc, wg_ref, bga, bgx, lam):
    xcb = xc.astype(bf16)
    half = D_RNN // 2
    g0 = _dot(xcb[:, :half], wg_ref[0])
    g1 = _dot(xcb[:, half:], wg_ref[1])
    ga = jnp.concatenate([g0[:, :half], g1[:, :half]], axis=1) + bga
    gx = jnp.concatenate([g0[:, half:], g1[:, half:]], axis=1) + bgx
    r = _sigmoid(ga)
    ig = _sigmoid(gx)
    log_a = (-RG_C) * r * _softplus(-lam)
    a = jnp.exp(log_a)
    u = jnp.sqrt(-jnp.tanh(log_a) * (1.0 + a * a)) * (ig * xc)
    return a, u


def _inproj_kernel(x_ref, nmix_ref, w_ref, bfc_ref, cw_ref, cb_ref, wg_ref, bga_ref, bgx_ref, lam_ref,
                   nrnn_ref, h0_ref, xr0_ref, c0_ref, km_ref, vm_ref,
                   q_ref, kT_ref, vT_ref, kTb_ref, vb_ref, lfT_ref, ca_ref, rn_ref, hlast_ref, xrlast_ref,
                   xp_scr, hc_scr, cc_scr, kc_scr, vc_scr):
    i = pl.program_id(1)
    nt = pl.num_programs(1) - 1
    head = TM - N_META

    @pl.when(i == 0)
    def _():
        xp_scr[0:SUBLANES, :] = xr0_ref[...]
        hc_scr[...] = h0_ref[SUBLANES - 1:SUBLANES, :]
        cc_scr[...] = c0_ref[:, N_META - 1:N_META]
        kc_scr[...] = km_ref[0:N_META, :]
        vc_scr[...] = vm_ref[0:N_META, :]

    @pl.when(i == nt)
    def _():
        pad = jnp.zeros((head, D_ATT), f32)
        kT_ref[0] = jnp.concatenate([kc_scr[...], pad], axis=0).T
        vT_ref[0] = jnp.concatenate([vc_scr[...], pad], axis=0).T

    @pl.when(i < nt)
    def _():
        hn = _rms(x_ref[0], nmix_ref[...]).astype(bf16)
        zr = _dot(hn, w_ref[:, 1536:2560])
        xr = zr[:, 0:512]
        gr = zr[:, 512:1024]
        zqk = _dot(hn, w_ref[:, 0:1024])

        xp_scr[SUBLANES:SUBLANES + TM, :] = xr
        cw = cw_ref[...]
        xc = (cb_ref[...] + cw[0:1] * xp_scr[5:5 + TM, :] + cw[1:2] * xp_scr[6:6 + TM, :]
              + cw[2:3] * xp_scr[7:7 + TM, :] + cw[3:4] * xr)
        xp_scr[0:SUBLANES, :] = xr[TM - SUBLANES:TM, :]
        xrlast_ref[0] = xr[TM - SUBLANES:TM, :]

        a, u = _gates(xc, wg_ref, bga_ref[...], bgx_ref[...], lam_ref[...])
        zv = _dot(hn, w_ref[:, 1024:1536])
        zf = _dot(hn, w_ref[:, ZF_COL:ZF_COL + LANES])

        q_ref[0] = (zqk[:, 0:512] * LOG2E).astype(bf16)
        k = zqk[:, 512:1024]
        kTb_ref[0] = k.T.astype(bf16)
        kT_ref[0] = jnp.concatenate([kc_scr[...], k[0:head]], axis=0).T
        kc_scr[...] = k[head:TM]

        hl = hc_scr[...]
        hs = []
        for r0 in range(0, TM, SUBLANES):
            ag, ug = _scan_rows(a[r0:r0 + SUBLANES], u[r0:r0 + SUBLANES])
            hg = ag * hl + ug
            hs.append(hg)
            hl = hg[SUBLANES - 1:SUBLANES, :]
        hc_scr[...] = hl
        hlast_ref[0] = hl

        vT_ref[0] = jnp.concatenate([vc_scr[...], zv[0:head]], axis=0).T
        vc_scr[...] = zv[head:TM]
        vb_ref[0] = zv.astype(bf16)
        lfT = _log_sigmoid(zf.T[0:N_HEADS, :] + bfc_ref[...])
        lfT_ref[0] = lfT
        cum = _cumsum_lanes(lfT) + cc_scr[...]
        ca_ref[0] = _bias_parts(cum)
        cc_scr[...] = cum[:, TM - 1:TM]

        y = jnp.concatenate(hs, axis=0) * _gelu(gr)
        rn_ref[0] = _rms(y, nrnn_ref[...]).astype(bf16)


def _inproj_call(x_real, nmix, w_cat, bf_col, cw, cb, wg, bga, bgx, lam, nrnn, h0, xr0, c0, km, vm):
    nb, t, _ = x_real.shape
    nt = t // TM
    last = nt - 1
    const = lambda shape: pl.BlockSpec(shape, lambda b, i: (0,) * len(shape), pipeline_mode=pl.Buffered(1))
    row_blk = lambda w: pl.BlockSpec((1, TM, w), lambda b, i: (b, jnp.minimum(i, last), 0))
    col_blk = lambda r: pl.BlockSpec((1, r, TM), lambda b, i: (b, 0, jnp.minimum(i, last)))
    full_col_blk = pl.BlockSpec((1, D_ATT, TM), lambda b, i: (b, 0, i))
    per_b = lambda r, w: pl.BlockSpec((1, r, w), lambda b, i: (b, 0, 0))
    return pl.pallas_call(
        _inproj_kernel,
        grid=(nb, nt + 1),
        in_specs=[row_blk(D_MODEL), const((1, D_MODEL)), const((D_MODEL, W_IN_COLS)), const((N_HEADS, 1)),
                  const((4, D_RNN)), const((1, D_RNN)), const((2, 256, 512)), const((1, D_RNN)),
                  const((1, D_RNN)), const((1, D_RNN)), const((1, D_RNN)),
                  const((SUBLANES, D_RNN)), const((SUBLANES, D_RNN)), const((N_HEADS, LANES)),
                  const((META_PAD, D_ATT)), const((META_PAD, D_ATT))],
        out_specs=[row_blk(D_ATT), full_col_blk, full_col_blk, col_blk(D_ATT), row_blk(D_ATT),
                   col_blk(N_HEADS), col_blk(BIAS_ROWS), row_blk(D_RNN), per_b(1, D_RNN), per_b(SUBLANES, D_RNN)],
        out_shape=[jax.ShapeDtypeStruct((nb, t, D_ATT), bf16),
                   jax.ShapeDtypeStruct((nb, D_ATT, N_META + t), f32),
                   jax.ShapeDtypeStruct((nb, D_ATT, N_META + t), f32),
                   jax.ShapeDtypeStruct((nb, D_ATT, t), bf16),
                   jax.ShapeDtypeStruct((nb, t, D_ATT), bf16),
                   jax.ShapeDtypeStruct((nb, N_HEADS, t), f32),
                   jax.ShapeDtypeStruct((nb, BIAS_ROWS, t), bf16),
                   jax.ShapeDtypeStruct((nb, t, D_RNN), bf16),
                   jax.ShapeDtypeStruct((nb, 1, D_RNN), f32),
                   jax.ShapeDtypeStruct((nb, SUBLANES, D_RNN), f32)],
        scratch_shapes=[pltpu.VMEM((TM + SUBLANES, D_RNN), f32),
                        pltpu.VMEM((1, D_RNN), f32), pltpu.VMEM((N_HEADS, 1), f32),
                        pltpu.VMEM((N_META, D_ATT), f32), pltpu.VMEM((N_META, D_ATT), f32)],
        compiler_params=pltpu.CompilerParams(dimension_semantics=("arbitrary", "arbitrary"),
                                             vmem_limit_bytes=VMEM_LIMIT),
        name="inproj_rglru",
    )(x_real, nmix, w_cat, bf_col, cw, cb, wg, bga, bgx, lam, nrnn, h0, xr0, c0, km, vm)


def _bias_parts(c):
    c2 = c * (-LOG2E)
    hi = c2.astype(bf16).astype(f32)
    r1 = c2 - hi
    mid = r1.astype(bf16).astype(f32)
    lo = (r1 - mid).astype(bf16).astype(f32)
    return jnp.concatenate([hi, mid, lo, jnp.zeros_like(hi)], axis=0).astype(bf16)


def _unit_softmax(j, c0, width, visible, s_scr, p_scr, m_scr, l_scr, acc_scr):
    for r0 in range(0, TQ, ROW_STRIP):
        rows = slice(r0, r0 + ROW_STRIP)
        st = []
        for t in range(width // LANES):
            vis = visible(r0, t)
            if vis is False:
                st.append(None)
                continue
            x = s_scr[j, rows, c0 + t * LANES:c0 + (t + 1) * LANES]
            st.append(x if vis is True else jnp.where(vis, x, -jnp.inf))
        live = [x for x in st if x is not None]
        mx = live[0]
        for x in live[1:]:
            mx = jnp.maximum(mx, x)
        m_old = m_scr[j, rows, :]
        m_new = jnp.maximum(m_old, jnp.max(mx, axis=1, keepdims=True))
        alpha = jnp.exp2(m_old - m_new)
        psum = None
        for t, x in enumerate(st):
            cols = slice(c0 + t * LANES, c0 + (t + 1) * LANES)
            if x is None:
                p_scr[j, rows, cols] = jnp.zeros((ROW_STRIP, LANES), bf16)
                continue
            p = jnp.exp2(x - m_new)
            psum = p if psum is None else psum + p
            p_scr[j, rows, cols] = p.astype(bf16)
        m_scr[j, rows, :] = m_new
        l_scr[j, rows, :] = alpha * l_scr[j, rows, :] + psum
        acc_scr[j, rows, :] = alpha * acc_scr[j, rows, :]


def _attn_units(units, qas, s_scr, p_scr, m_scr, l_scr, acc_scr):
    def scores(u):
        j, c0, width, kTa, _, _ = u
        s_scr[j, :, c0:c0 + width] = _dot(qas[j], kTa)

    def values(u):
        j, c0, width, _, v, _ = u
        acc_scr[j] += _dot(p_scr[j, :, c0:c0 + width], v)

    for u in units[:2]:
        scores(u)
    for n, u in enumerate(units):
        j, c0, width, _, _, visible = u
        _unit_softmax(j, c0, width, visible, s_scr, p_scr, m_scr, l_scr, acc_scr)
        if n + 2 < len(units):
            scores(units[n + 2])
        values(u)


def _attn_kernel(q_ref, kT_ref, v_ref, ca_ref, kTm_ref, vm_ref, cam_ref, o_ref, s_scr, p_scr, m_scr, l_scr, acc_scr):
    hp = pl.program_id(1)
    i = pl.program_id(2)
    q = q_ref[0].astype(f32)
    lane = lax.broadcasted_iota(jnp.int32, (TQ, LANES), 1)
    qas = []
    for j in range(2):
        h = 2 * hp + j
        qj = jnp.where((lane >= j * HEAD_DIM) & (lane < (j + 1) * HEAD_DIM), q, 0.0)
        pick = jnp.where((lane == h) | (lane == h + N_HEADS) | (lane == h + 2 * N_HEADS), 1.0, 0.0)
        qas.append(jnp.concatenate([qj, pick], axis=1).astype(bf16))
    scr = (s_scr, p_scr, m_scr, l_scr, acc_scr)
    m_scr[...] = jnp.full(m_scr.shape, -jnp.inf, f32)
    l_scr[...] = jnp.zeros(l_scr.shape, f32)
    acc_scr[...] = jnp.zeros(acc_scr.shape, f32)

    def keys_aug(k0):
        pad = jnp.zeros((LANES - BIAS_ROWS, TK), bf16)
        return jnp.concatenate([kT_ref[0, :, pl.ds(k0, TK)], ca_ref[0, :, pl.ds(k0, TK)], pad], axis=0)

    def tile_units(k0, n_tiles):
        units = []
        for a in range(n_tiles):
            ka = pl.multiple_of(k0 + a * TK, TK)
            kTa = keys_aug(ka)
            v = v_ref[0, pl.ds(ka, TK), :]
            units += [(j, a * TK, TK, kTa, v, lambda r0, t: True) for j in range(2)]
        return units

    def body(kt, carry):
        _attn_units(tile_units(kt * (ATTN_TRIP * TK), ATTN_TRIP), qas, *scr)
        return carry

    lax.fori_loop(0, i // ATTN_TRIP, body, 0)

    slane = lax.broadcasted_iota(jnp.int32, (ROW_STRIP, LANES), 1)
    srow = lax.broadcasted_iota(jnp.int32, (ROW_STRIP, LANES), 0)

    def last_visible(r0, t):
        if t == 0:
            return slane < N_META
        c0 = (t - 1) * LANES
        if c0 + LANES - 1 <= r0:
            return True
        if c0 > r0 + ROW_STRIP - 1:
            return False
        return slane + c0 <= srow + r0

    def last_units(col0):
        k0 = pl.multiple_of(i * TQ, TQ)
        meta_aug = jnp.concatenate([kTm_ref[...], cam_ref[...],
                                    jnp.zeros((LANES - BIAS_ROWS, META_PAD), bf16)], axis=0)
        kTa = jnp.concatenate([meta_aug, keys_aug(k0)], axis=1)
        v = jnp.concatenate([vm_ref[...], v_ref[0, pl.ds(k0, TK), :]], axis=0)
        return [(j, col0, META_PAD + TK, kTa, v, last_visible) for j in range(2)]

    for rem in range(ATTN_TRIP):
        @pl.when(i % ATTN_TRIP == rem)
        def _(rem=rem):
            _attn_units(tile_units((i - rem) * TK, rem) + last_units(rem * TK), qas, *scr)

    outs = [acc_scr[j] / jnp.sum(l_scr[j], axis=1, keepdims=True) for j in range(2)]
    o_ref[0] = jnp.where(lane < HEAD_DIM, outs[0], outs[1])


def _attn_call(q, kTb, vb, caug, kTm, vm, caugm):
    nb, t, _ = q.shape
    npair = N_HEADS // 2
    return pl.pallas_call(
        _attn_kernel,
        grid=(nb, npair, t // TQ),
        in_specs=[pl.BlockSpec((1, TQ, LANES), lambda b, h, i: (b, i, h)),
                  pl.BlockSpec((1, LANES, t), lambda b, h, i: (b, h, 0)),
                  pl.BlockSpec((1, t, LANES), lambda b, h, i: (b, 0, h)),
                  pl.BlockSpec((1, BIAS_ROWS, t), lambda b, h, i: (b, 0, 0)),
                  pl.BlockSpec((LANES, META_PAD), lambda b, h, i: (h, 0)),
                  pl.BlockSpec((META_PAD, LANES), lambda b, h, i: (0, h)),
                  pl.BlockSpec((BIAS_ROWS, META_PAD), lambda b, h, i: (0, 0))],
        out_specs=pl.BlockSpec((1, TQ, LANES), lambda b, h, i: (b, i, h)),
        out_shape=jax.ShapeDtypeStruct((nb, t, D_ATT), f32),
        scratch_shapes=[pltpu.VMEM((2, TQ, ATTN_COLS), f32), pltpu.VMEM((2, TQ, ATTN_COLS), bf16),
                        pltpu.VMEM((2, TQ, LANES), f32), pltpu.VMEM((2, TQ, LANES), f32),
                        pltpu.VMEM((2, TQ, LANES), f32)],
        compiler_params=pltpu.CompilerParams(dimension_semantics=("arbitrary", "arbitrary", "arbitrary"),
                                             vmem_limit_bytes=VMEM_LIMIT),
        name="prompt_attention",
    )(q, kTb, vb, caug, kTm, vm, caugm)


def _ffn_kernel(x_ref, att_ref, rn_ref, natt_ref, woa_ref, wor_ref, nffn_ref, wup_ref, fw_ref, fb_ref,
                wdn_ref, nfin_ref, gp0_ref, y_ref, gplast_ref, gp_scr):
    i = pl.program_id(1)

    @pl.when(i == 0)
    def _():
        gp_scr[0:SUBLANES, :] = gp0_ref[...]

    an = _rms(att_ref[0], natt_ref[...]).astype(bf16)
    x1 = x_ref[0] + _dot(an, woa_ref[...]) + _dot(rn_ref[0], wor_ref[...])
    hn = _rms(x1, nffn_ref[...]).astype(bf16)
    down = None
    for c0, wd in FF_CHUNKS:
        cols = slice(c0, c0 + wd)
        gp = _dot(hn, wup_ref[:, cols])
        val = _dot(hn, wup_ref[:, D_FF + c0:D_FF + c0 + wd])
        gp_scr[SUBLANES:SUBLANES + TM, cols] = gp
        fw = fw_ref[:, cols]
        g = (fb_ref[:, cols] + fw[0:1] * gp_scr[6:6 + TM, cols] + fw[1:2] * gp_scr[7:7 + TM, cols]
             + fw[2:3] * gp)
        act = (_gelu(g) * val).astype(bf16)
        part = _dot(act, wdn_ref[cols, :])
        down = part if down is None else down + part
    last = gp_scr[TM:TM + SUBLANES, :]
    gp_scr[0:SUBLANES, :] = last
    gplast_ref[0] = last
    y_ref[0] = _rms(x1 + down, nfin_ref[...])


def _ffn_call(x_real, att, rn, natt, woa, wor, nffn, wup, fw, fb, wdn, nfin, gp0):
    nb, t, _ = x_real.shape
    const = lambda shape: pl.BlockSpec(shape, lambda b, i: (0,) * len(shape), pipeline_mode=pl.Buffered(1))
    row_blk = lambda w: pl.BlockSpec((1, TM, w), lambda b, i: (b, i, 0))
    return pl.pallas_call(
        _ffn_kernel,
        grid=(nb, t // TM),
        in_specs=[row_blk(D_MODEL), row_blk(D_ATT), row_blk(D_RNN), const((1, D_ATT)),
                  const((D_ATT, D_MODEL)), const((D_RNN, D_MODEL)), const((1, D_MODEL)),
                  const((D_MODEL, 2 * D_FF)), const((3, D_FF)), const((1, D_FF)),
                  const((D_FF, D_MODEL)), const((1, D_MODEL)), const((SUBLANES, D_FF))],
        out_specs=[row_blk(D_MODEL), pl.BlockSpec((1, SUBLANES, D_FF), lambda b, i: (b, 0, 0))],
        out_shape=[jax.ShapeDtypeStruct((nb, t, D_MODEL), f32),
                   jax.ShapeDtypeStruct((nb, SUBLANES, D_FF), f32)],
        scratch_shapes=[pltpu.VMEM((TM + SUBLANES, D_FF), f32)],
        compiler_params=pltpu.CompilerParams(dimension_semantics=("arbitrary", "arbitrary"),
                                             vmem_limit_bytes=VMEM_LIMIT),
        name="outproj_ffn",
    )(x_real, att, rn, natt, woa, wor, nffn, wup, fw, fb, wdn, nfin, gp0)


def _small_pre_kernel(x_ref, nmix_ref, w_ref, bfc_ref, bfr_ref, cw_ref, cb_ref, wg_ref, bga_ref, bgx_ref,
                      lam_ref, nrnn_ref, hs0_ref, cs0_ref, cs1_ref, cs2_ref,
                      qs_ref, ks_ref, vs_ref, lfs_ref, hs_ref, xrs_ref,
                      kTm_ref, vTm_ref, km_ref, vm_ref, lfTm_ref, cTm_ref, cam_ref, attm_ref, hm_ref, xrm_ref, rn_ref):
    m0 = META_PAD
    hn = _rms(x_ref[...], nmix_ref[...]).astype(bf16)
    z = _dot(hn, w_ref[...])
    cw = cw_ref[...]
    lam = lam_ref[...]

    zs = z[m0:, :]
    qs_ref[...] = zs[:, 0:512]
    ks_ref[...] = zs[:, 512:1024]
    vs_ref[...] = zs[:, 1024:1536]
    lfs_ref[...] = _log_sigmoid(zs[:, ZF_COL:ZF_COL + N_HEADS] + bfr_ref[...])
    xrs = zs[:, 1536:2048]
    xrs_ref[...] = xrs
    xcs = cb_ref[...] + cw[0:1] * cs0_ref[...] + cw[1:2] * cs1_ref[...] + cw[2:3] * cs2_ref[...] + cw[3:4] * xrs
    a_s, u_s = _gates(xcs, wg_ref, bga_ref[...], bgx_ref[...], lam)
    h_s = a_s * hs0_ref[...] + u_s
    hs_ref[...] = h_s
    rn_ref[m0:, :] = _rms(h_s * _gelu(zs[:, 2048:2560]), nrnn_ref[...])

    zm = z[0:m0, :]
    km = zm[:, 512:1024]
    vm = zm[:, 1024:1536]
    kTm = km.T
    kTm_ref[...] = kTm
    vTm_ref[...] = vm.T
    km_ref[...] = km
    vm_ref[...] = vm
    lfT = _log_sigmoid(zm[:, ZF_COL:ZF_COL + LANES].T[0:N_HEADS, :] + bfc_ref[...])
    lfTm_ref[...] = lfT
    cT = _cumsum_lanes(lfT)
    cTm_ref[...] = cT
    cam_ref[...] = _bias_parts(cT)

    xrm = zm[:, 1536:2048]
    xrm_ref[...] = xrm[SUBLANES:2 * SUBLANES, :]
    xcm = (cb_ref[...] + cw[0:1] * _shift_rows(xrm, 3, 0.0) + cw[1:2] * _shift_rows(xrm, 2, 0.0)
           + cw[2:3] * _shift_rows(xrm, 1, 0.0) + cw[3:4] * xrm)
    a_m, u_m = _gates(xcm[0:N_META], wg_ref, bga_ref[...], bgx_ref[...], lam)
    _, h_m = _scan_rows(a_m, u_m)
    hm_ref[...] = h_m[SUBLANES:2 * SUBLANES, :]
    rn_ref[0:N_META, :] = _rms(h_m * _gelu(zm[0:N_META, 2048:2560]), nrnn_ref[...])
    rn_ref[N_META:m0, :] = jnp.zeros((m0 - N_META, D_RNN), f32)

    qm = zm[:, 0:512].astype(bf16)
    kTb = kTm.astype(bf16)
    vb = vm.astype(bf16)
    lane = lax.broadcasted_iota(jnp.int32, (m0, LANES), 1)
    row = lax.broadcasted_iota(jnp.int32, (m0, LANES), 0)
    mask = (lane <= row) & (lane < N_META)
    for hp in range(N_HEADS // 2):
        sl = slice(hp * LANES, (hp + 1) * LANES)
        qp = qm[:, sl]
        outs = []
        for j in range(2):
            h = 2 * hp + j
            qj = jnp.where((lane >= j * HEAD_DIM) & (lane < (j + 1) * HEAD_DIM), qp, jnp.zeros_like(qp))
            s = _dot(qj, kTb[sl, :]) - cT[h:h + 1, :]
            s = jnp.where(mask, s, -jnp.inf)
            p = jnp.exp(s - jnp.max(s, axis=1, keepdims=True))
            o = _dot(p.astype(bf16), vb[:, sl]) / jnp.sum(p, axis=1, keepdims=True)
            outs.append(o)
        attm_ref[:, sl] = jnp.where(lane < HEAD_DIM, outs[0], outs[1])


def _small_pre_call(xs, nmix, w_cat, bf_col, bf_row, cw, cb, wg, bga, bgx, lam, nrnn, hs0, cs0, cs1, cs2):
    ns = SMALL_ROWS - META_PAD
    shp = jax.ShapeDtypeStruct
    return pl.pallas_call(
        _small_pre_kernel,
        out_shape=[shp((ns, D_ATT), f32), shp((ns, D_ATT), f32), shp((ns, D_ATT), f32), shp((ns, N_HEADS), f32),
                   shp((ns, D_RNN), f32), shp((ns, D_RNN), f32),
                   shp((D_ATT, META_PAD), f32), shp((D_ATT, META_PAD), f32),
                   shp((META_PAD, D_ATT), f32), shp((META_PAD, D_ATT), f32),
                   shp((N_HEADS, META_PAD), f32), shp((N_HEADS, META_PAD), f32), shp((BIAS_ROWS, META_PAD), bf16),
                   shp((META_PAD, D_ATT), f32), shp((SUBLANES, D_RNN), f32), shp((SUBLANES, D_RNN), f32),
                   shp((SMALL_ROWS, D_RNN), f32)],
        compiler_params=pltpu.CompilerParams(vmem_limit_bytes=VMEM_LIMIT),
        name="small_pre",
    )(xs, nmix, w_cat, bf_col, bf_row, cw, cb, wg, bga, bgx, lam, nrnn, hs0, cs0, cs1, cs2)


def _small_post_kernel(x_ref, att_ref, rn_ref, natt_ref, woa_ref, wor_ref, nffn_ref, wup_ref, fw_ref, fb_ref,
                       wdn_ref, nfin_ref, f0_ref, f1_ref, ys_ref, gps_ref, gpm_ref):
    m0 = META_PAD
    an = _rms(att_ref[...], natt_ref[...]).astype(bf16)
    x1 = x_ref[...] + _dot(an, woa_ref[...]) + _dot(rn_ref[...].astype(bf16), wor_ref[...])
    hn = _rms(x1, nffn_ref[...]).astype(bf16)
    up = _dot(hn, wup_ref[...])
    gp = up[:, 0:D_FF]
    val = up[:, D_FF:]
    fw = fw_ref[...]
    gpm = gp[0:m0]
    gm = fb_ref[...] + fw[0:1] * _shift_rows(gpm, 2, 0.0) + fw[1:2] * _shift_rows(gpm, 1, 0.0) + fw[2:3] * gpm
    gps = gp[m0:]
    gs = fb_ref[...] + fw[0:1] * f0_ref[...] + fw[1:2] * f1_ref[...] + fw[2:3] * gps
    gps_ref[...] = gps
    gpm_ref[...] = gpm[SUBLANES:2 * SUBLANES, :]
    g = jnp.concatenate([gm, gs], axis=0)
    act = (_gelu(g) * val).astype(bf16)
    x2 = x1 + _dot(act, wdn_ref[...])
    ys_ref[...] = _rms(x2[m0:], nfin_ref[...])


def _small_post_call(xs, att, rn, natt, woa, wor, nffn, wup, fw, fb, wdn, nfin, f0, f1):
    ns = SMALL_ROWS - META_PAD
    shp = jax.ShapeDtypeStruct
    return pl.pallas_call(
        _small_post_kernel,
        out_shape=[shp((ns, D_MODEL), f32), shp((ns, D_FF), f32), shp((SUBLANES, D_FF), f32)],
        compiler_params=pltpu.CompilerParams(vmem_limit_bytes=VMEM_LIMIT),
        name="small_post",
    )(xs, att, rn, natt, woa, wor, nffn, wup, fw, fb, wdn, nfin, f0, f1)


def _paged_kernel(pt_ref, qbc_ref, q_ref, kn_ref, vn_ref, lfn_ref, ck_ref, cv_ref, cl_ref, o_ref,
                  buf, lbuf, s_scr, acc_scr, sem, lsem, *, n_seq, n_pages):
    b = pl.program_id(0)
    n_chunks = n_pages // PAGES_PER_CHUNK
    per_seq = 2 * n_chunks
    assert per_seq % N_SLOTS == 0 and n_pages % PAGES_PER_CHUNK == 0

    def chunk_copies(seq, c, slot):
        src = ck_ref if c < n_chunks else cv_ref
        p0 = (c % n_chunks) * PAGES_PER_CHUNK
        return [pltpu.make_async_copy(src.at[pt_ref[seq, p0 + pg]], buf.at[slot, pg], sem.at[slot])
                for pg in range(PAGES_PER_CHUNK)]

    def logf_copies(seq, slot):
        return [pltpu.make_async_copy(cl_ref.at[pt_ref[seq, pg]], lbuf.at[slot, pg], lsem.at[slot])
                for pg in range(n_pages)]

    lslot = b % 2
    nxt = jnp.minimum(b + 1, n_seq - 1)

    @pl.when(b == 0)
    def _():
        for cp in logf_copies(0, 0):
            cp.start()
        for c in range(N_SLOTS - 1):
            for cp in chunk_copies(0, c, c):
                cp.start()

    @pl.when(b + 1 < n_seq)
    def _():
        for cp in logf_copies(nxt, 1 - lslot):
            cp.start()

    acc_scr[...] = jnp.zeros(acc_scr.shape, f32)

    for c in range(per_seq):
        ahead = c + N_SLOTS - 1
        if ahead < per_seq:
            for cp in chunk_copies(b, ahead, ahead % N_SLOTS):
                cp.start()
        else:
            @pl.when(b + 1 < n_seq)
            def _():
                for cp in chunk_copies(nxt, ahead - per_seq, ahead % N_SLOTS):
                    cp.start()
        slot = c % N_SLOTS
        for cp in chunk_copies(b, c, slot):
            cp.wait()

        if c < n_chunks:
            for h in range(N_HEADS):
                qh = qbc_ref[0, h]
                for pg in range(PAGES_PER_CHUNK):
                    red = jnp.sum(buf[slot, pg, h] * qh, axis=0, keepdims=True)
                    s_scr[c * PAGES_PER_CHUNK + pg, h:h + 1, :] = red
        else:
            for h in range(N_HEADS):
                a = jnp.zeros((HEAD_DIM, PAGE), f32)
                for pg in range(PAGES_PER_CHUNK):
                    page = (c - n_chunks) * PAGES_PER_CHUNK + pg
                    a = a + buf[slot, pg, h] * s_scr[page, h:h + 1, :]
                acc_scr[h] += a

        if c == n_chunks - 1:
            for cp in logf_copies(b, lslot):
                cp.wait()
            rows = n_pages * N_HEADS
            lf = lbuf[lslot].reshape(rows, PAGE)
            cw = _cumsum_lanes(lf)
            tot = jnp.broadcast_to(cw[:, PAGE - 1:PAGE], (rows, PAGE))
            ridx = lax.broadcasted_iota(jnp.int32, (rows, PAGE), 0)
            off = tot
            sh = N_HEADS
            while sh < rows:
                off = off + jnp.where(ridx >= sh, pltpu.roll(off, sh, 0), 0.0)
                sh *= 2
            cfull = cw + (off - tot)
            s3 = (s_scr[...].reshape(rows, PAGE) - cfull).reshape(n_pages, N_HEADS, PAGE)
            c_past = off[rows - N_HEADS:rows, 0:1]
            s_new = jnp.sum(q_ref[0] * kn_ref[0], axis=1, keepdims=True) - (c_past + lfn_ref[0])
            m = jnp.max(jnp.max(s3, axis=0), axis=1, keepdims=True)
            m = jnp.maximum(m, s_new)
            p3 = jnp.exp(s3 - m[None])
            p_new = jnp.exp(s_new - m)
            l = jnp.sum(jnp.sum(p3, axis=0), axis=1, keepdims=True) + p_new
            s_scr[...] = p3

    num = jnp.sum(acc_scr[...], axis=2) + p_new * vn_ref[0]
    o_ref[0] = num / l


def _paged_call(page_table, qbc, q3, kn3, vn3, lfn3, ckT, cvT, clT):
    n_seq, n_pages = page_table.shape
    blk3 = lambda w: pl.BlockSpec((1, N_HEADS, w), lambda b, pt: (b, 0, 0))
    grid_spec = pltpu.PrefetchScalarGridSpec(
        num_scalar_prefetch=1,
        grid=(n_seq,),
        in_specs=[pl.BlockSpec((1, N_HEADS, HEAD_DIM, PAGE), lambda b, pt: (b, 0, 0, 0)),
                  blk3(HEAD_DIM), blk3(HEAD_DIM), blk3(HEAD_DIM), blk3(1),
                  pl.BlockSpec(memory_space=pl.ANY), pl.BlockSpec(memory_space=pl.ANY),
                  pl.BlockSpec(memory_space=pl.ANY)],
        out_specs=blk3(HEAD_DIM),
        scratch_shapes=[pltpu.VMEM((N_SLOTS, PAGES_PER_CHUNK, N_HEADS, HEAD_DIM, PAGE), f32),
                        pltpu.VMEM((2, n_pages, N_HEADS, PAGE), f32),
                        pltpu.VMEM((n_pages, N_HEADS, PAGE), f32),
                        pltpu.VMEM((N_HEADS, HEAD_DIM, PAGE), f32),
                        pltpu.SemaphoreType.DMA((N_SLOTS,)), pltpu.SemaphoreType.DMA((2,))])
    return pl.pallas_call(
        functools.partial(_paged_kernel, n_seq=n_seq, n_pages=n_pages),
        grid_spec=grid_spec,
        out_shape=jax.ShapeDtypeStruct((n_seq, N_HEADS, HEAD_DIM), f32),
        compiler_params=pltpu.CompilerParams(dimension_semantics=("arbitrary",), vmem_limit_bytes=VMEM_LIMIT),
        name="paged_attention",
    )(page_table, qbc, q3, kn3, vn3, lfn3, ckT, cvT, clT)


def _block_diag(w):
    n, r, c = w.shape
    eye = jnp.eye(n, dtype=w.dtype)
    return (eye[:, None, :, None] * w[:, :, None, :]).reshape(n * r, n * c)


def kernel(x_prompt, x_sample, cache_k, cache_v, cache_logf, state_rnn_h, state_rnn_conv, state_ffn_conv, page_table, meta_tokens, norm_mix, w_in, b_forget, rnn_conv_w, rnn_conv_b, w_rg_a, b_rg_a, w_rg_x, b_rg_x, rg_lambda, norm_att_out, norm_rnn_out, w_out, norm_ffn, w_ffn_up, ffn_conv_w, ffn_conv_b, w_ffn_down, norm_final):
    nb, seq, _ = x_prompt.shape
    db = x_sample.shape[0]
    assert norm_mix.shape[0] == 1 and x_sample.shape[1] == 1 and seq % TM == 0 and seq % TQ == 0

    w = w_in[0]
    wf = jnp.pad(w[:, 3 * D_ATT:3 * D_ATT + N_HEADS], ((0, 0), (0, LANES - N_HEADS)))
    w_cat = jnp.concatenate([w[:, :D_ATT] * ATT_SCALE, w[:, D_ATT:3 * D_ATT], w[:, 3 * D_ATT + N_HEADS:], wf],
                            axis=1).astype(bf16)
    bda = _block_diag(w_rg_a[0])
    bdx = _block_diag(w_rg_x[0])
    wg = jnp.stack([jnp.concatenate([bda[:256, :256], bdx[:256, :256]], axis=1),
                    jnp.concatenate([bda[256:, 256:], bdx[256:, 256:]], axis=1)]).astype(bf16)
    woa = w_out[0, :D_ATT].astype(bf16)
    wor = w_out[0, D_ATT:].astype(bf16)
    wup = w_ffn_up[0].astype(bf16)
    wdn = w_ffn_down[0].astype(bf16)
    bf_col = b_forget[0][:, None]
    bf_row = b_forget
    nfin = norm_final[None, :]
    mix_args = (norm_mix, w_cat, bf_col)
    rnn_args = (rnn_conv_w[0], rnn_conv_b, wg, b_rg_a, b_rg_x, rg_lambda, norm_rnn_out)
    post_args = (norm_att_out, woa, wor, norm_ffn, wup, ffn_conv_w[0], ffn_conv_b, wdn, nfin)

    xs = jnp.concatenate([meta_tokens, jnp.zeros((META_PAD - N_META, D_MODEL), f32), x_sample[:, 0, :]], axis=0)
    (q_s, k_s, v_s, lf_s, h_s, xr_s, kTm, vTm, km, vm, lfTm, cTm, caug_m, att_m, h_m, xr_m, rn_small) = _small_pre_call(
        xs, *mix_args, bf_row, *rnn_args, state_rnn_h[0],
        state_rnn_conv[0, :, 0], state_rnn_conv[0, :, 1], state_rnn_conv[0, :, 2])

    ckT = jnp.transpose(cache_k[0], (0, 2, 3, 1))
    cvT = jnp.transpose(cache_v[0], (0, 2, 3, 1))
    clT = jnp.transpose(cache_logf[0], (0, 2, 1))
    q3 = q_s.reshape(db, N_HEADS, HEAD_DIM)
    qbc = jnp.broadcast_to(q3[:, :, :, None], (db, N_HEADS, HEAD_DIM, PAGE))
    att_s = _paged_call(page_table, qbc, q3, k_s.reshape(db, N_HEADS, HEAD_DIM),
                        v_s.reshape(db, N_HEADS, HEAD_DIM), lf_s[:, :, None], ckT, cvT, clT)

    att_small = jnp.concatenate([att_m, att_s.reshape(db, D_ATT)], axis=0)
    y_s, gp_s, gp_m = _small_post_call(xs, att_small, rn_small, *post_args,
                                       state_ffn_conv[0, :, 0], state_ffn_conv[0, :, 1])

    (q, kT, vT, kTb, vb, lfT, caug, rn, h_last, xr_last) = _inproj_call(
        x_prompt, *mix_args, *rnn_args, h_m, xr_m, cTm, km, vm)
    att = _attn_call(q, kTb, vb, caug, kTm.astype(bf16), vm.astype(bf16), caug_m)
    y_prompt, gp_last = _ffn_call(x_prompt, att, rn, *post_args, gp_m)

    def with_meta(meta_t, real_t):
        c = meta_t.shape[0]
        return jnp.concatenate([jnp.broadcast_to(meta_t[None, :, :N_META], (nb, c, N_META)), real_t], axis=2)

    t_all = N_META + seq
    k_prompt = jnp.transpose(kT.reshape(nb, N_HEADS, HEAD_DIM, t_all), (0, 3, 1, 2))[None]
    v_prompt = jnp.transpose(vT.reshape(nb, N_HEADS, HEAD_DIM, t_all), (0, 3, 1, 2))[None]
    logf_prompt = jnp.transpose(with_meta(lfTm, lfT), (0, 2, 1))[None]
    rnn_h_prompt = h_last[:, 0, :][None]
    rnn_conv_prompt = xr_last[:, SUBLANES - 3:, :][None]
    ffn_conv_prompt = gp_last[:, SUBLANES - 2:, :][None]

    k_sample = k_s.reshape(1, db, 1, N_HEADS, HEAD_DIM)
    v_sample = v_s.reshape(1, db, 1, N_HEADS, HEAD_DIM)
    logf_sample = lf_s.reshape(1, db, 1, N_HEADS)
    rnn_h_sample = h_s[None]
    rnn_conv_sample = jnp.stack([state_rnn_conv[0, :, 1], state_rnn_conv[0, :, 2], xr_s], axis=1)[None]
    ffn_conv_sample = jnp.stack([state_ffn_conv[0, :, 1], gp_s], axis=1)[None]
    y_sample = y_s[:, None, :]

    return (y_prompt, y_sample, k_prompt, v_prompt, logf_prompt, rnn_h_prompt, rnn_conv_prompt, ffn_conv_prompt,
            k_sample, v_sample, logf_sample, rnn_h_sample, rnn_conv_sample, ffn_conv_sample)
```

```python
import functools

import jax
import jax.numpy as jnp
from jax import lax
from jax.experimental import pallas as pl
from jax.experimental.pallas import tpu as pltpu

D_MODEL = 1024
N_META = 16
D_ATT = 512
D_RNN = 512
HEAD_DIM = 64
N_HEADS = 8
N_RNN_BLOCKS = 8
RG_C = 8.0
D_FF = 2816
EPS = 1e-6
ATT_SCALE = HEAD_DIM ** -0.5
PAGE = 128

LANES = 128
SUBLANES = 8
ZF_COL = 5 * 512
W_IN_COLS = ZF_COL + LANES
META_PAD = 128
SMALL_ROWS = META_PAD + 32
VMEM_LIMIT = 60 * 1024 * 1024

TM = 512
TQ = 512
TK = 512
ROW_STRIP = 32
ATTN_TRIP = 4
ATTN_COLS = ATTN_TRIP * TK + META_PAD
BIAS_ROWS = 32
LOG2E = 1.4426950408889634
FF_CHUNKS = ((0, 1536), (1536, 1280))
PAGES_PER_CHUNK = 8
N_SLOTS = 4
FF_W = 256

bf16 = jnp.bfloat16
f32 = jnp.float32


def _rms(x, g):
    return x * lax.rsqrt(jnp.mean(x * x, axis=-1, keepdims=True) + EPS) * g


def _gelu(x):
    return x * (0.5 * (1.0 + jnp.tanh(0.7978845608028654 * (x + 0.044715 * (x * x * x)))))


def _sigmoid(x):
    return 0.5 * (1.0 + jnp.tanh(0.5 * x))


def _softplus(x):
    return jnp.maximum(x, 0.0) + jnp.log1p(jnp.exp(-jnp.abs(x)))


def _log_sigmoid(x):
    return -_softplus(-x)


def _dot(a, b):
    return jnp.dot(a, b, preferred_element_type=f32)


def _shift_rows(x, s, fill):
    row = lax.broadcasted_iota(jnp.int32, x.shape, 0)
    return jnp.where(row >= s, pltpu.roll(x, s, 0), fill)


def _cumsum_lanes(x):
    n = x.shape[-1]
    lane = lax.broadcasted_iota(jnp.int32, x.shape, x.ndim - 1)
    s = 1
    while s < n:
        x = x + jnp.where(lane >= s, pltpu.roll(x, s, x.ndim - 1), 0.0)
        s *= 2
    return x


def _scan_rows(a, u):
    n = a.shape[0]
    s = 1
    while s < n:
        a_sh = _shift_rows(a, s, 1.0)
        u_sh = _shift_rows(u, s, 0.0)
        u = a * u_sh + u
        a = a * a_sh
        s *= 2
    return a, u


def _gates(xc, wg_ref, bga, bgx, lam):
    xcb = xc.astype(bf16)
    half = D_RNN // 2
    g0 = _dot(xcb[:, :half], wg_ref[0])
    g1 = _dot(xcb[:, half:], wg_ref[1])
    ga = jnp.concatenate([g0[:, :half], g1[:, :half]], axis=1) + bga
    gx = jnp.concatenate([g0[:, half:], g1[:, half:]], axis=1) + bgx
    r = _sigmoid(ga)
    ig = _sigmoid(gx)
    log_a = (-RG_C) * r * _softplus(-lam)
    a = jnp.exp(log_a)
    u = jnp.sqrt(-jnp.tanh(log_a) * (1.0 + a * a)) * (ig * xc)
    return a, u


def _inproj_kernel(x_ref, nmix_ref, w_ref, bfc_ref, cw_ref, cb_ref, wg_ref, bga_ref, bgx_ref, lam_ref,
                   nrnn_ref, h0_ref, xr0_ref, c0_ref, km_ref, vm_ref,
                   q_ref, kT_ref, vT_ref, kTb_ref, vb_ref, lfT_ref, ca_ref, rn_ref, hlast_ref, xrlast_ref,
                   xp_scr, hc_scr, cc_scr, kc_scr, vc_scr):
    i = pl.program_id(1)
    nt = pl.num_programs(1) - 1
    head = TM - N_META

    @pl.when(i == 0)
    def _():
        xp_scr[0:SUBLANES, :] = xr0_ref[...]
        hc_scr[...] = h0_ref[SUBLANES - 1:SUBLANES, :]
        cc_scr[...] = c0_ref[:, N_META - 1:N_META]
        kc_scr[...] = km_ref[0:N_META, :]
        vc_scr[...] = vm_ref[0:N_META, :]

    @pl.when(i == nt)
    def _():
        pad = jnp.zeros((head, D_ATT), f32)
        kT_ref[0] = jnp.concatenate([kc_scr[...], pad], axis=0).T
        vT_ref[0] = jnp.concatenate([vc_scr[...], pad], axis=0).T

    @pl.when(i < nt)
    def _():
        hn = _rms(x_ref[0], nmix_ref[...]).astype(bf16)
        zr = _dot(hn, w_ref[:, 1536:2560])
        xr = zr[:, 0:512]
        gr = zr[:, 512:1024]
        zqk = _dot(hn, w_ref[:, 0:1024])

        xp_scr[SUBLANES:SUBLANES + TM, :] = xr
        cw = cw_ref[...]
        xc = (cb_ref[...] + cw[0:1] * xp_scr[5:5 + TM, :] + cw[1:2] * xp_scr[6:6 + TM, :]
              + cw[2:3] * xp_scr[7:7 + TM, :] + cw[3:4] * xr)
        xp_scr[0:SUBLANES, :] = xr[TM - SUBLANES:TM, :]
        xrlast_ref[0] = xr[TM - SUBLANES:TM, :]

        a, u = _gates(xc, wg_ref, bga_ref[...], bgx_ref[...], lam_ref[...])
        zv = _dot(hn, w_ref[:, 1024:1536])
        zf = _dot(hn, w_ref[:, ZF_COL:ZF_COL + LANES])

        q_ref[0] = (zqk[:, 0:512] * LOG2E).astype(bf16)
        k = zqk[:, 512:1024]
        kTb_ref[0] = k.T.astype(bf16)
        kT_ref[0] = jnp.concatenate([kc_scr[...], k[0:head]], axis=0).T
        kc_scr[...] = k[head:TM]

        hl = hc_scr[...]
        hs = []
        for r0 in range(0, TM, SUBLANES):
            ag, ug = _scan_rows(a[r0:r0 + SUBLANES], u[r0:r0 + SUBLANES])
            hg = ag * hl + ug
            hs.append(hg)
            hl = hg[SUBLANES - 1:SUBLANES, :]
        hc_scr[...] = hl
        hlast_ref[0] = hl

        vT_ref[0] = jnp.concatenate([vc_scr[...], zv[0:head]], axis=0).T
        vc_scr[...] = zv[head:TM]
        vb_ref[0] = zv.astype(bf16)
        lfT = _log_sigmoid(zf.T[0:N_HEADS, :] + bfc_ref[...])
        lfT_ref[0] = lfT
        cum = _cumsum_lanes(lfT) + cc_scr[...]
        ca_ref[0] = _bias_parts(cum)
        cc_scr[...] = cum[:, TM - 1:TM]

        y = jnp.concatenate(hs, axis=0) * _gelu(gr)
        rn_ref[0] = _rms(y, nrnn_ref[...]).astype(bf16)


def _inproj_call(x_real, nmix, w_cat, bf_col, cw, cb, wg, bga, bgx, lam, nrnn, h0, xr0, c0, km, vm):
    nb, t, _ = x_real.shape
    nt = t // TM
    last = nt - 1
    const = lambda shape: pl.BlockSpec(shape, lambda b, i: (0,) * len(shape), pipeline_mode=pl.Buffered(1))
    row_blk = lambda w: pl.BlockSpec((1, TM, w), lambda b, i: (b, jnp.minimum(i, last), 0))
    col_blk = lambda r: pl.BlockSpec((1, r, TM), lambda b, i: (b, 0, jnp.minimum(i, last)))
    full_col_blk = pl.BlockSpec((1, D_ATT, TM), lambda b, i: (b, 0, i))
    per_b = lambda r, w: pl.BlockSpec((1, r, w), lambda b, i: (b, 0, 0))
    return pl.pallas_call(
        _inproj_kernel,
        grid=(nb, nt + 1),
        in_specs=[row_blk(D_MODEL), const((1, D_MODEL)), const((D_MODEL, W_IN_COLS)), const((N_HEADS, 1)),
                  const((4, D_RNN)), const((1, D_RNN)), const((2, 256, 512)), const((1, D_RNN)),
                  const((1, D_RNN)), const((1, D_RNN)), const((1, D_RNN)),
                  const((SUBLANES, D_RNN)), const((SUBLANES, D_RNN)), const((N_HEADS, LANES)),
                  const((META_PAD, D_ATT)), const((META_PAD, D_ATT))],
        out_specs=[row_blk(D_ATT), full_col_blk, full_col_blk, col_blk(D_ATT), row_blk(D_ATT),
                   col_blk(N_HEADS), col_blk(BIAS_ROWS), row_blk(D_RNN), per_b(1, D_RNN), per_b(SUBLANES, D_RNN)],
        out_shape=[jax.ShapeDtypeStruct((nb, t, D_ATT), bf16),
                   jax.ShapeDtypeStruct((nb, D_ATT, N_META + t), f32),
                   jax.ShapeDtypeStruct((nb, D_ATT, N_META + t), f32),
                   jax.ShapeDtypeStruct((nb, D_ATT, t), bf16),
                   jax.ShapeDtypeStruct((nb, t, D_ATT), bf16),
                   jax.ShapeDtypeStruct((nb, N_HEADS, t), f32),
                   jax.ShapeDtypeStruct((nb, BIAS_ROWS, t), bf16),
                   jax.ShapeDtypeStruct((nb, t, D_RNN), bf16),
                   jax.ShapeDtypeStruct((nb, 1, D_RNN), f32),
                   jax.ShapeDtypeStruct((nb, SUBLANES, D_RNN), f32)],
        scratch_shapes=[pltpu.VMEM((TM + SUBLANES, D_RNN), f32),
                        pltpu.VMEM((1, D_RNN), f32), pltpu.VMEM((N_HEADS, 1), f32),
                        pltpu.VMEM((N_META, D_ATT), f32), pltpu.VMEM((N_META, D_ATT), f32)],
        compiler_params=pltpu.CompilerParams(dimension_semantics=("arbitrary", "arbitrary"),
                                             vmem_limit_bytes=VMEM_LIMIT),
        name="inproj_rglru",
    )(x_real, nmix, w_cat, bf_col, cw, cb, wg, bga, bgx, lam, nrnn, h0, xr0, c0, km, vm)


def _bias_parts(c):
    c2 = c * (-LOG2E)
    hi = c2.astype(bf16).astype(f32)
    r1 = c2 - hi
    mid = r1.astype(bf16).astype(f32)
    lo = (r1 - mid).astype(bf16).astype(f32)
    return jnp.concatenate([hi, mid, lo, jnp.zeros_like(hi)], axis=0).astype(bf16)


def _unit_softmax(j, c0, width, visible, s_scr, p_scr, m_scr, l_scr, acc_scr):
    for r0 in range(0, TQ, ROW_STRIP):
        rows = slice(r0, r0 + ROW_STRIP)
        st = []
        for t in range(width // LANES):
            vis = visible(r0, t)
            if vis is False:
                st.append(None)
                continue
            x = s_scr[j, rows, c0 + t * LANES:c0 + (t + 1) * LANES]
            st.append(x if vis is True else jnp.where(vis, x, -jnp.inf))
        live = [x for x in st if x is not None]
        mx = live[0]
        for x in live[1:]:
            mx = jnp.maximum(mx, x)
        m_old = m_scr[j, rows, :]
        m_new = jnp.maximum(m_old, jnp.max(mx, axis=1, keepdims=True))
        alpha = jnp.exp2(m_old - m_new)
        psum = None
        for t, x in enumerate(st):
            cols = slice(c0 + t * LANES, c0 + (t + 1) * LANES)
            if x is None:
                p_scr[j, rows, cols] = jnp.zeros((ROW_STRIP, LANES), bf16)
                continue
            p = jnp.exp2(x - m_new)
            psum = p if psum is None else psum + p
            p_scr[j, rows, cols] = p.astype(bf16)
        m_scr[j, rows, :] = m_new
        l_scr[j, rows, :] = alpha * l_scr[j, rows, :] + psum
        acc_scr[j, rows, :] = alpha * acc_scr[j, rows, :]


def _attn_units(units, qas, s_scr, p_scr, m_scr, l_scr, acc_scr):
    def scores(u):
        j, c0, width, kTa, _, _ = u
        s_scr[j, :, c0:c0 + width] = _dot(qas[j], kTa)

    def values(u):
        j, c0, width, _, v, _ = u
        acc_scr[j] += _dot(p_scr[j, :, c0:c0 + width], v)

    for u in units[:2]:
        scores(u)
    for n, u in enumerate(units):
        j, c0, width, _, _, visible = u
        _unit_softmax(j, c0, width, visible, s_scr, p_scr, m_scr, l_scr, acc_scr)
        if n + 2 < len(units):
            scores(units[n + 2])
        values(u)


def _attn_kernel(q_ref, kT_ref, v_ref, ca_ref, kTm_ref, vm_ref, cam_ref, o_ref, s_scr, p_scr, m_scr, l_scr, acc_scr):
    hp = pl.program_id(1)
    i = pl.program_id(2)
    q = q_ref[0].astype(f32)
    lane = lax.broadcasted_iota(jnp.int32, (TQ, LANES), 1)
    qas = []
    for j in range(2):
        h = 2 * hp + j
        qj = jnp.where((lane >= j * HEAD_DIM) & (lane < (j + 1) * HEAD_DIM), q, 0.0)
        pick = jnp.where((lane == h) | (lane == h + N_HEADS) | (lane == h + 2 * N_HEADS), 1.0, 0.0)
        qas.append(jnp.concatenate([qj, pick], axis=1).astype(bf16))
    scr = (s_scr, p_scr, m_scr, l_scr, acc_scr)
    m_scr[...] = jnp.full(m_scr.shape, -jnp.inf, f32)
    l_scr[...] = jnp.zeros(l_scr.shape, f32)
    acc_scr[...] = jnp.zeros(acc_scr.shape, f32)

    def keys_aug(k0):
        pad = jnp.zeros((LANES - BIAS_ROWS, TK), bf16)
        return jnp.concatenate([kT_ref[0, :, pl.ds(k0, TK)], ca_ref[0, :, pl.ds(k0, TK)], pad], axis=0)

    def tile_units(k0, n_tiles):
        units = []
        for a in range(n_tiles):
            ka = pl.multiple_of(k0 + a * TK, TK)
            kTa = keys_aug(ka)
            v = v_ref[0, pl.ds(ka, TK), :]
            units += [(j, a * TK, TK, kTa, v, lambda r0, t: True) for j in range(2)]
        return units

    def body(kt, carry):
        _attn_units(tile_units(kt * (ATTN_TRIP * TK), ATTN_TRIP), qas, *scr)
        return carry

    lax.fori_loop(0, i // ATTN_TRIP, body, 0)

    slane = lax.broadcasted_iota(jnp.int32, (ROW_STRIP, LANES), 1)
    srow = lax.broadcasted_iota(jnp.int32, (ROW_STRIP, LANES), 0)

    def last_visible(r0, t):
        if t == 0:
            return slane < N_META
        c0 = (t - 1) * LANES
        if c0 + LANES - 1 <= r0:
            return True
        if c0 > r0 + ROW_STRIP - 1:
            return False
        return slane + c0 <= srow + r0

    def last_units(col0):
        k0 = pl.multiple_of(i * TQ, TQ)
        meta_aug = jnp.concatenate([kTm_ref[...], cam_ref[...],
                                    jnp.zeros((LANES - BIAS_ROWS, META_PAD), bf16)], axis=0)
        kTa = jnp.concatenate([meta_aug, keys_aug(k0)], axis=1)
        v = jnp.concatenate([vm_ref[...], v_ref[0, pl.ds(k0, TK), :]], axis=0)
        return [(j, col0, META_PAD + TK, kTa, v, last_visible) for j in range(2)]

    for rem in range(ATTN_TRIP):
        @pl.when(i % ATTN_TRIP == rem)
        def _(rem=rem):
            _attn_units(tile_units((i - rem) * TK, rem) + last_units(rem * TK), qas, *scr)

    outs = [acc_scr[j] / jnp.sum(l_scr[j], axis=1, keepdims=True) for j in range(2)]
    o_ref[0] = jnp.where(lane < HEAD_DIM, outs[0], outs[1])


def _attn_call(q, kTb, vb, caug, kTm, vm, caugm):
    nb, t, _ = q.shape
    npair = N_HEADS // 2
    return pl.pallas_call(
        _attn_kernel,
        grid=(nb, npair, t // TQ),
        in_specs=[pl.BlockSpec((1, TQ, LANES), lambda b, h, i: (b, i, h)),
                  pl.BlockSpec((1, LANES, t), lambda b, h, i: (b, h, 0)),
                  pl.BlockSpec((1, t, LANES), lambda b, h, i: (b, 0, h)),
                  pl.BlockSpec((1, BIAS_ROWS, t), lambda b, h, i: (b, 0, 0)),
                  pl.BlockSpec((LANES, META_PAD), lambda b, h, i: (h, 0)),
                  pl.BlockSpec((META_PAD, LANES), lambda b, h, i: (0, h)),
                  pl.BlockSpec((BIAS_ROWS, META_PAD), lambda b, h, i: (0, 0))],
        out_specs=pl.BlockSpec((1, TQ, LANES), lambda b, h, i: (b, i, h)),
        out_shape=jax.ShapeDtypeStruct((nb, t, D_ATT), f32),
        scratch_shapes=[pltpu.VMEM((2, TQ, ATTN_COLS), f32), pltpu.VMEM((2, TQ, ATTN_COLS), bf16),
                        pltpu.VMEM((2, TQ, LANES), f32), pltpu.VMEM((2, TQ, LANES), f32),
                        pltpu.VMEM((2, TQ, LANES), f32)],
        compiler_params=pltpu.CompilerParams(dimension_semantics=("arbitrary", "arbitrary", "arbitrary"),
                                             vmem_limit_bytes=VMEM_LIMIT),
        name="prompt_attention",
    )(q, kTb, vb, caug, kTm, vm, caugm)


def _ffn_kernel(x_ref, att_ref, rn_ref, natt_ref, woa_ref, wor_ref, nffn_ref, wup_ref, fw_ref, fb_ref,
                wdn_ref, nfin_ref, gp0_ref, y_ref, gplast_ref, gp_scr):
    i = pl.program_id(1)

    @pl.when(i == 0)
    def _():
        gp_scr[0:SUBLANES, :] = gp0_ref[...]

    an = _rms(att_ref[0], natt_ref[...]).astype(bf16)
    x1 = x_ref[0] + _dot(an, woa_ref[...]) + _dot(rn_ref[0], wor_ref[...])
    hn = _rms(x1, nffn_ref[...]).astype(bf16)
    down = None
    for c0, wd in FF_CHUNKS:
        cols = slice(c0, c0 + wd)
        gp = _dot(hn, wup_ref[:, cols])
        val = _dot(hn, wup_ref[:, D_FF + c0:D_FF + c0 + wd])
        gp_scr[SUBLANES:SUBLANES + TM, cols] = gp
        fw = fw_ref[:, cols]
        g = (fb_ref[:, cols] + fw[0:1] * gp_scr[6:6 + TM, cols] + fw[1:2] * gp_scr[7:7 + TM, cols]
             + fw[2:3] * gp)
        act = (_gelu(g) * val).astype(bf16)
        part = _dot(act, wdn_ref[cols, :])
        down = part if down is None else down + part
    last = gp_scr[TM:TM + SUBLANES, :]
    gp_scr[0:SUBLANES, :] = last
    gplast_ref[0] = last
    y_ref[0] = _rms(x1 + down, nfin_ref[...])


def _ffn_call(x_real, att, rn, natt, woa, wor, nffn, wup, fw, fb, wdn, nfin, gp0):
    nb, t, _ = x_real.shape
    const = lambda shape: pl.BlockSpec(shape, lambda b, i: (0,) * len(shape), pipeline_mode=pl.Buffered(1))
    row_blk = lambda w: pl.BlockSpec((1, TM, w), lambda b, i: (b, i, 0))
    return pl.pallas_call(
        _ffn_kernel,
        grid=(nb, t // TM),
        in_specs=[row_blk(D_MODEL), row_blk(D_ATT), row_blk(D_RNN), const((1, D_ATT)),
                  const((D_ATT, D_MODEL)), const((D_RNN, D_MODEL)), const((1, D_MODEL)),
                  const((D_MODEL, 2 * D_FF)), const((3, D_FF)), const((1, D_FF)),
                  const((D_FF, D_MODEL)), const((1, D_MODEL)), const((SUBLANES, D_FF))],
        out_specs=[row_blk(D_MODEL), pl.BlockSpec((1, SUBLANES, D_FF), lambda b, i: (b, 0, 0))],
        out_shape=[jax.ShapeDtypeStruct((nb, t, D_MODEL), f32),
                   jax.ShapeDtypeStruct((nb, SUBLANES, D_FF), f32)],
        scratch_shapes=[pltpu.VMEM((TM + SUBLANES, D_FF), f32)],
        compiler_params=pltpu.CompilerParams(dimension_semantics=("arbitrary", "arbitrary"),
                                             vmem_limit_bytes=VMEM_LIMIT),
        name="outproj_ffn",
    )(x_real, att, rn, natt, woa, wor, nffn, wup, fw, fb, wdn, nfin, gp0)


def _small_pre_kernel(x_ref, nmix_ref, w_ref, bfc_ref, bfr_ref, cw_ref, cb_ref, wg_ref, bga_ref, bgx_ref,
                      lam_ref, nrnn_ref, hs0_ref, cs0_ref, cs1_ref, cs2_ref,
                      qs_ref, ks_ref, vs_ref, lfs_ref, hs_ref, xrs_ref,
                      kTm_ref, vTm_ref, km_ref, vm_ref, lfTm_ref, cTm_ref, cam_ref, attm_ref, hm_ref, xrm_ref, rn_ref):
    m0 = META_PAD
    hn = _rms(x_ref[...], nmix_ref[...]).astype(bf16)
    z = _dot(hn, w_ref[...])
    cw = cw_ref[...]
    lam = lam_ref[...]

    zs = z[m0:, :]
    qs_ref[...] = zs[:, 0:512]
    ks_ref[...] = zs[:, 512:1024]
    vs_ref[...] = zs[:, 1024:1536]
    lfs_ref[...] = _log_sigmoid(zs[:, ZF_COL:ZF_COL + N_HEADS] + bfr_ref[...])
    xrs = zs[:, 1536:2048]
    xrs_ref[...] = xrs
    xcs = cb_ref[...] + cw[0:1] * cs0_ref[...] + cw[1:2] * cs1_ref[...] + cw[2:3] * cs2_ref[...] + cw[3:4] * xrs
    a_s, u_s = _gates(xcs, wg_ref, bga_ref[...], bgx_ref[...], lam)
    h_s = a_s * hs0_ref[...] + u_s
    hs_ref[...] = h_s
    rn_ref[m0:, :] = _rms(h_s * _gelu(zs[:, 2048:2560]), nrnn_ref[...])

    zm = z[0:m0, :]
    km = zm[:, 512:1024]
    vm = zm[:, 1024:1536]
    kTm = km.T
    kTm_ref[...] = kTm
    vTm_ref[...] = vm.T
    km_ref[...] = km
    vm_ref[...] = vm
    lfT = _log_sigmoid(zm[:, ZF_COL:ZF_COL + LANES].T[0:N_HEADS, :] + bfc_ref[...])
    lfTm_ref[...] = lfT
    cT = _cumsum_lanes(lfT)
    cTm_ref[...] = cT
    cam_ref[...] = _bias_parts(cT)

    xrm = zm[:, 1536:2048]
    xrm_ref[...] = xrm[SUBLANES:2 * SUBLANES, :]
    xcm = (cb_ref[...] + cw[0:1] * _shift_rows(xrm, 3, 0.0) + cw[1:2] * _shift_rows(xrm, 2, 0.0)
           + cw[2:3] * _shift_rows(xrm, 1, 0.0) + cw[3:4] * xrm)
    a_m, u_m = _gates(xcm[0:N_META], wg_ref, bga_ref[...], bgx_ref[...], lam)
    _, h_m = _scan_rows(a_m, u_m)
    hm_ref[...] = h_m[SUBLANES:2 * SUBLANES, :]
    rn_ref[0:N_META, :] = _rms(h_m * _gelu(zm[0:N_META, 2048:2560]), nrnn_ref[...])
    rn_ref[N_META:m0, :] = jnp.zeros((m0 - N_META, D_RNN), f32)

    qm = zm[:, 0:512].astype(bf16)
    kTb = kTm.astype(bf16)
    vb = vm.astype(bf16)
    lane = lax.broadcasted_iota(jnp.int32, (m0, LANES), 1)
    row = lax.broadcasted_iota(jnp.int32, (m0, LANES), 0)
    mask = (lane <= row) & (lane < N_META)
    for hp in range(N_HEADS // 2):
        sl = slice(hp * LANES, (hp + 1) * LANES)
        qp = qm[:, sl]
        outs = []
        for j in range(2):
            h = 2 * hp + j
            qj = jnp.where((lane >= j * HEAD_DIM) & (lane < (j + 1) * HEAD_DIM), qp, jnp.zeros_like(qp))
            s = _dot(qj, kTb[sl, :]) - cT[h:h + 1, :]
            s = jnp.where(mask, s, -jnp.inf)
            p = jnp.exp(s - jnp.max(s, axis=1, keepdims=True))
            o = _dot(p.astype(bf16), vb[:, sl]) / jnp.sum(p, axis=1, keepdims=True)
            outs.append(o)
        attm_ref[:, sl] = jnp.where(lane < HEAD_DIM, outs[0], outs[1])


def _small_pre_call(xs, nmix, w_cat, bf_col, bf_row, cw, cb, wg, bga, bgx, lam, nrnn, hs0, cs0, cs1, cs2):
    ns = SMALL_ROWS - META_PAD
    shp = jax.ShapeDtypeStruct
    return pl.pallas_call(
        _small_pre_kernel,
        out_shape=[shp((ns, D_ATT), f32), shp((ns, D_ATT), f32), shp((ns, D_ATT), f32), shp((ns, N_HEADS), f32),
                   shp((ns, D_RNN), f32), shp((ns, D_RNN), f32),
                   shp((D_ATT, META_PAD), f32), shp((D_ATT, META_PAD), f32),
                   shp((META_PAD, D_ATT), f32), shp((META_PAD, D_ATT), f32),
                   shp((N_HEADS, META_PAD), f32), shp((N_HEADS, META_PAD), f32), shp((BIAS_ROWS, META_PAD), bf16),
                   shp((META_PAD, D_ATT), f32), shp((SUBLANES, D_RNN), f32), shp((SUBLANES, D_RNN), f32),
                   shp((SMALL_ROWS, D_RNN), f32)],
        compiler_params=pltpu.CompilerParams(vmem_limit_bytes=VMEM_LIMIT),
        name="small_pre",
    )(xs, nmix, w_cat, bf_col, bf_row, cw, cb, wg, bga, bgx, lam, nrnn, hs0, cs0, cs1, cs2)


def _small_post_kernel(x_ref, att_ref, rn_ref, natt_ref, woa_ref, wor_ref, nffn_ref, wup_ref, fw_ref, fb_ref,
                       wdn_ref, nfin_ref, f0_ref, f1_ref, ys_ref, gps_ref, gpm_ref):
    m0 = META_PAD
    an = _rms(att_ref[...], natt_ref[...]).astype(bf16)
    x1 = x_ref[...] + _dot(an, woa_ref[...]) + _dot(rn_ref[...].astype(bf16), wor_ref[...])
    hn = _rms(x1, nffn_ref[...]).astype(bf16)
    up = _dot(hn, wup_ref[...])
    gp = up[:, 0:D_FF]
    val = up[:, D_FF:]
    fw = fw_ref[...]
    gpm = gp[0:m0]
    gm = fb_ref[...] + fw[0:1] * _shift_rows(gpm, 2, 0.0) + fw[1:2] * _shift_rows(gpm, 1, 0.0) + fw[2:3] * gpm
    gps = gp[m0:]
    gs = fb_ref[...] + fw[0:1] * f0_ref[...] + fw[1:2] * f1_ref[...] + fw[2:3] * gps
    gps_ref[...] = gps
    gpm_ref[...] = gpm[SUBLANES:2 * SUBLANES, :]
    g = jnp.concatenate([gm, gs], axis=0)
    act = (_gelu(g) * val).astype(bf16)
    x2 = x1 + _dot(act, wdn_ref[...])
    ys_ref[...] = _rms(x2[m0:], nfin_ref[...])


def _small_post_call(xs, att, rn, natt, woa, wor, nffn, wup, fw, fb, wdn, nfin, f0, f1):
    ns = SMALL_ROWS - META_PAD
    shp = jax.ShapeDtypeStruct
    return pl.pallas_call(
        _small_post_kernel,
        out_shape=[shp((ns, D_MODEL), f32), shp((ns, D_FF), f32), shp((SUBLANES, D_FF), f32)],
        compiler_params=pltpu.CompilerParams(vmem_limit_bytes=VMEM_LIMIT),
        name="small_post",
    )(xs, att, rn, natt, woa, wor, nffn, wup, fw, fb, wdn, nfin, f0, f1)


def _paged_kernel(pt_ref, qbc_ref, q_ref, kn_ref, vn_ref, lfn_ref, ck_ref, cv_ref, cl_ref, o_ref,
                  buf, lbuf, s_scr, acc_scr, sem, lsem, *, n_seq, n_pages):
    b = pl.program_id(0)
    n_chunks = n_pages // PAGES_PER_CHUNK
    per_seq = 2 * n_chunks
    assert per_seq % N_SLOTS == 0 and n_pages % PAGES_PER_CHUNK == 0

    def chunk_copies(seq, c, slot):
        src = ck_ref if c < n_chunks else cv_ref
        p0 = (c % n_chunks) * PAGES_PER_CHUNK
        return [pltpu.make_async_copy(src.at[pt_ref[seq, p0 + pg]], buf.at[slot, pg], sem.at[slot])
                for pg in range(PAGES_PER_CHUNK)]

    def logf_copies(seq, slot):
        return [pltpu.make_async_copy(cl_ref.at[pt_ref[seq, pg]], lbuf.at[slot, pg], lsem.at[slot])
                for pg in range(n_pages)]

    lslot = b % 2
    nxt = jnp.minimum(b + 1, n_seq - 1)

    @pl.when(b == 0)
    def _():
        for cp in logf_copies(0, 0):
            cp.start()
        for c in range(N_SLOTS - 1):
            for cp in chunk_copies(0, c, c):
                cp.start()

    @pl.when(b + 1 < n_seq)
    def _():
        for cp in logf_copies(nxt, 1 - lslot):
            cp.start()

    acc_scr[...] = jnp.zeros(acc_scr.shape, f32)

    for c in range(per_seq):
        ahead = c + N_SLOTS - 1
        if ahead < per_seq:
            for cp in chunk_copies(b, ahead, ahead % N_SLOTS):
                cp.start()
        else:
            @pl.when(b + 1 < n_seq)
            def _():
                for cp in chunk_copies(nxt, ahead - per_seq, ahead % N_SLOTS):
                    cp.start()
        slot = c % N_SLOTS
        for cp in chunk_copies(b, c, slot):
            cp.wait()

        if c < n_chunks:
            for h in range(N_HEADS):
                qh = qbc_ref[0, h]
                for pg in range(PAGES_PER_CHUNK):
                    red = jnp.sum(buf[slot, pg, h] * qh, axis=0, keepdims=True)
                    s_scr[c * PAGES_PER_CHUNK + pg, h:h + 1, :] = red
        else:
            for h in range(N_HEADS):
                a = jnp.zeros((HEAD_DIM, PAGE), f32)
                for pg in range(PAGES_PER_CHUNK):
                    page = (c - n_chunks) * PAGES_PER_CHUNK + pg
                    a = a + buf[slot, pg, h] * s_scr[page, h:h + 1, :]
                acc_scr[h] += a

        if c == n_chunks - 1:
            for cp in logf_copies(b, lslot):
                cp.wait()
            rows = n_pages * N_HEADS
            lf = lbuf[lslot].reshape(rows, PAGE)
            cw = _cumsum_lanes(lf)
            tot = jnp.broadcast_to(cw[:, PAGE - 1:PAGE], (rows, PAGE))
            ridx = lax.broadcasted_iota(jnp.int32, (rows, PAGE), 0)
            off = tot
            sh = N_HEADS
            while sh < rows:
                off = off + jnp.where(ridx >= sh, pltpu.roll(off, sh, 0), 0.0)
                sh *= 2
            cfull = cw + (off - tot)
            s3 = (s_scr[...].reshape(rows, PAGE) - cfull).reshape(n_pages, N_HEADS, PAGE)
            c_past = off[rows - N_HEADS:rows, 0:1]
            s_new = jnp.sum(q_ref[0] * kn_ref[0], axis=1, keepdims=True) - (c_past + lfn_ref[0])
            m = jnp.max(jnp.max(s3, axis=0), axis=1, keepdims=True)
            m = jnp.maximum(m, s_new)
            p3 = jnp.exp(s3 - m[None])
            p_new = jnp.exp(s_new - m)
            l = jnp.sum(jnp.sum(p3, axis=0), axis=1, keepdims=True) + p_new
            s_scr[...] = p3

    num = jnp.sum(acc_scr[...], axis=2) + p_new * vn_ref[0]
    o_ref[0] = num / l


def _paged_call(page_table, qbc, q3, kn3, vn3, lfn3, ckT, cvT, clT):
    n_seq, n_pages = page_table.shape
    blk3 = lambda w: pl.BlockSpec((1, N_HEADS, w), lambda b, pt: (b, 0, 0))
    grid_spec = pltpu.PrefetchScalarGridSpec(
        num_scalar_prefetch=1,
        grid=(n_seq,),
        in_specs=[pl.BlockSpec((1, N_HEADS, HEAD_DIM, PAGE), lambda b, pt: (b, 0, 0, 0)),
                  blk3(HEAD_DIM), blk3(HEAD_DIM), blk3(HEAD_DIM), blk3(1),
                  pl.BlockSpec(memory_space=pl.ANY), pl.BlockSpec(memory_space=pl.ANY),
                  pl.BlockSpec(memory_space=pl.ANY)],
        out_specs=blk3(HEAD_DIM),
        scratch_shapes=[pltpu.VMEM((N_SLOTS, PAGES_PER_CHUNK, N_HEADS, HEAD_DIM, PAGE), f32),
                        pltpu.VMEM((2, n_pages, N_HEADS, PAGE), f32),
                        pltpu.VMEM((n_pages, N_HEADS, PAGE), f32),
                        pltpu.VMEM((N_HEADS, HEAD_DIM, PAGE), f32),
                        pltpu.SemaphoreType.DMA((N_SLOTS,)), pltpu.SemaphoreType.DMA((2,))])
    return pl.pallas_call(
        functools.partial(_paged_kernel, n_seq=n_seq, n_pages=n_pages),
        grid_spec=grid_spec,
        out_shape=jax.ShapeDtypeStruct((n_seq, N_HEADS, HEAD_DIM), f32),
        compiler_params=pltpu.CompilerParams(dimension_semantics=("arbitrary",), vmem_limit_bytes=VMEM_LIMIT),
        name="paged_attention",
    )(page_table, qbc, q3, kn3, vn3, lfn3, ckT, cvT, clT)


def _paged_sequence(seq, pt_ref, qbc_ref, q_ref, kn_ref, vn_ref, lfn_ref, ck_ref, cv_ref, cl_ref, o_ref,
                    buf, lbuf, s_scr, acc_scr, sem, lsem, n_seq, n_pages, interleave):
    n_chunks = n_pages // PAGES_PER_CHUNK
    per_seq = 2 * n_chunks
    assert per_seq % N_SLOTS == 0 and n_pages % PAGES_PER_CHUNK == 0

    def chunk_copies(sq, c, slot):
        src = ck_ref if c < n_chunks else cv_ref
        p0 = (c % n_chunks) * PAGES_PER_CHUNK
        return [pltpu.make_async_copy(src.at[pt_ref[sq, p0 + pg]], buf.at[slot, pg], sem.at[slot])
                for pg in range(PAGES_PER_CHUNK)]

    def logf_copies(sq, slot):
        return [pltpu.make_async_copy(cl_ref.at[pt_ref[sq, pg]], lbuf.at[slot, pg], lsem.at[slot])
                for pg in range(n_pages)]

    lslot = seq % 2
    nxt = jnp.minimum(seq + 1, n_seq - 1)

    @pl.when(seq == 0)
    def _():
        for cp in logf_copies(0, 0):
            cp.start()
        for c in range(N_SLOTS - 1):
            for cp in chunk_copies(0, c, c):
                cp.start()

    @pl.when(seq + 1 < n_seq)
    def _():
        for cp in logf_copies(nxt, 1 - lslot):
            cp.start()

    acc_scr[...] = jnp.zeros(acc_scr.shape, f32)

    for c in range(per_seq):
        ahead = c + N_SLOTS - 1
        if ahead < per_seq:
            for cp in chunk_copies(seq, ahead, ahead % N_SLOTS):
                cp.start()
        else:
            @pl.when(seq + 1 < n_seq)
            def _():
                for cp in chunk_copies(nxt, ahead - per_seq, ahead % N_SLOTS):
                    cp.start()
        slot = c % N_SLOTS
        for cp in chunk_copies(seq, c, slot):
            cp.wait()

        if c < n_chunks:
            for h in range(N_HEADS):
                qh = qbc_ref[0, h]
                for pg in range(PAGES_PER_CHUNK):
                    red = jnp.sum(buf[slot, pg, h] * qh, axis=0, keepdims=True)
                    s_scr[c * PAGES_PER_CHUNK + pg, h:h + 1, :] = red
        else:
            for h in range(N_HEADS):
                a = jnp.zeros((HEAD_DIM, PAGE), f32)
                for pg in range(PAGES_PER_CHUNK):
                    page = (c - n_chunks) * PAGES_PER_CHUNK + pg
                    a = a + buf[slot, pg, h] * s_scr[page, h:h + 1, :]
                acc_scr[h] += a

        if c == n_chunks - 1:
            for cp in logf_copies(seq, lslot):
                cp.wait()
            rows = n_pages * N_HEADS
            lf = lbuf[lslot].reshape(rows, PAGE)
            cw = _cumsum_lanes(lf)
            tot = jnp.broadcast_to(cw[:, PAGE - 1:PAGE], (rows, PAGE))
            ridx = lax.broadcasted_iota(jnp.int32, (rows, PAGE), 0)
            off = tot
            sh = N_HEADS
            while sh < rows:
                off = off + jnp.where(ridx >= sh, pltpu.roll(off, sh, 0), 0.0)
                sh *= 2
            cfull = cw + (off - tot)
            s3 = (s_scr[...].reshape(rows, PAGE) - cfull).reshape(n_pages, N_HEADS, PAGE)
            c_past = off[rows - N_HEADS:rows, 0:1]
            s_new = jnp.sum(q_ref[0] * kn_ref[0], axis=1, keepdims=True) - (c_past + lfn_ref[0])
            m = jnp.max(jnp.max(s3, axis=0), axis=1, keepdims=True)
            m = jnp.maximum(m, s_new)
            p3 = jnp.exp(s3 - m[None])
            p_new = jnp.exp(s_new - m)
            l = jnp.sum(jnp.sum(p3, axis=0), axis=1, keepdims=True) + p_new
            s_scr[...] = p3

        interleave(c)

    num = jnp.sum(acc_scr[...], axis=2) + p_new * vn_ref[0]
    o_ref[0] = num / l


def _ffn_paged_kernel(pt_ref, x_ref, att_ref, rn_ref, natt_ref, woa_ref, wor_ref, nffn_ref, wup_ref, fw_ref,
                      fb_ref, wdn_ref, nfin_ref, gp0_ref, qbc_ref, q_ref, kn_ref, vn_ref, lfn_ref,
                      ck_ref, cv_ref, cl_ref,
                      y_ref, gplast_ref, o_ref,
                      carry_scr, conv_scr, val_scr, act_scr, x1_scr, hn_scr, down_scr,
                      buf, lbuf, s_scr, acc_scr, sem, lsem,
                      *, n_seq, n_pages):
    i = pl.program_id(1)
    seq = pl.program_id(0) * pl.num_programs(1) + i
    n_ff = D_FF // FF_W

    @pl.when(i == 0)
    def _():
        carry_scr[...] = gp0_ref[...]

    an = _rms(att_ref[0], natt_ref[...]).astype(bf16)
    x1 = x_ref[0] + _dot(an, woa_ref[...]) + _dot(rn_ref[0], wor_ref[...])
    x1_scr[...] = x1
    hn_scr[...] = _rms(x1, nffn_ref[...]).astype(bf16)

    def cols(c):
        return slice(c * FF_W, (c + 1) * FF_W)

    def up_proj(c):
        hn = hn_scr[...]
        conv_scr[c % 2, SUBLANES:SUBLANES + TM, :] = _dot(hn, wup_ref[:, cols(c)])
        val_scr[c % 2] = _dot(hn, wup_ref[:, D_FF + c * FF_W:D_FF + (c + 1) * FF_W])

    def gate(c):
        conv_scr[c % 2, 0:SUBLANES, :] = carry_scr[:, cols(c)]
        gp = conv_scr[c % 2, SUBLANES:SUBLANES + TM, :]
        fw = fw_ref[:, cols(c)]
        g = (fb_ref[:, cols(c)] + fw[0:1] * conv_scr[c % 2, 6:6 + TM, :] + fw[1:2] * conv_scr[c % 2, 7:7 + TM, :]
             + fw[2:3] * gp)
        carry_scr[:, cols(c)] = gp[TM - SUBLANES:TM, :]
        act_scr[c % 2] = (_gelu(g) * val_scr[c % 2]).astype(bf16)

    def down_proj(c):
        part = _dot(act_scr[c % 2], wdn_ref[cols(c), :])
        if c == 0:
            down_scr[...] = part
        else:
            down_scr[...] += part

    def ffn_segment(s):
        if s + 1 < n_ff:
            up_proj(s + 1)
        if 1 <= s <= n_ff:
            down_proj(s - 1)
        if s < n_ff:
            gate(s)
        if s == n_ff + 1:
            gplast_ref[0] = carry_scr[...]
            y_ref[0] = _rms(x1_scr[...] + down_scr[...], nfin_ref[...])

    up_proj(0)
    _paged_sequence(seq, pt_ref, qbc_ref, q_ref, kn_ref, vn_ref, lfn_ref, ck_ref, cv_ref, cl_ref, o_ref,
                    buf, lbuf, s_scr, acc_scr, sem, lsem, n_seq, n_pages, ffn_segment)


def _ffn_paged_call(page_table, x_real, att, rn, natt, woa, wor, nffn, wup, fw, fb, wdn, nfin, gp0,
                    qbc, q3, kn3, vn3, lfn3, ckT, cvT, clT):
    nb, t, _ = x_real.shape
    nt = t // TM
    n_seq, n_pages = page_table.shape
    assert n_seq == nb * nt and 2 * (n_pages // PAGES_PER_CHUNK) > D_FF // FF_W + 1 and D_FF % FF_W == 0
    const = lambda shape: pl.BlockSpec(shape, lambda b, i, pt: (0,) * len(shape), pipeline_mode=pl.Buffered(1))
    row_blk = lambda w: pl.BlockSpec((1, TM, w), lambda b, i, pt: (b, i, 0))
    seq_blk = lambda *shape: pl.BlockSpec((1,) + shape, lambda b, i, pt: (b * nt + i,) + (0,) * len(shape))
    hbm = pl.BlockSpec(memory_space=pl.ANY)
    grid_spec = pltpu.PrefetchScalarGridSpec(
        num_scalar_prefetch=1,
        grid=(nb, nt),
        in_specs=[row_blk(D_MODEL), row_blk(D_ATT), row_blk(D_RNN), const((1, D_ATT)),
                  const((D_ATT, D_MODEL)), const((D_RNN, D_MODEL)), const((1, D_MODEL)),
                  const((D_MODEL, 2 * D_FF)), const((3, D_FF)), const((1, D_FF)),
                  const((D_FF, D_MODEL)), const((1, D_MODEL)), const((SUBLANES, D_FF)),
                  seq_blk(N_HEADS, HEAD_DIM, PAGE), seq_blk(N_HEADS, HEAD_DIM), seq_blk(N_HEADS, HEAD_DIM),
                  seq_blk(N_HEADS, HEAD_DIM), seq_blk(N_HEADS, 1), hbm, hbm, hbm],
        out_specs=[row_blk(D_MODEL), pl.BlockSpec((1, SUBLANES, D_FF), lambda b, i, pt: (b, 0, 0)),
                   seq_blk(N_HEADS, HEAD_DIM)],
        scratch_shapes=[pltpu.VMEM((SUBLANES, D_FF), f32), pltpu.VMEM((2, TM + SUBLANES, FF_W), f32),
                        pltpu.VMEM((2, TM, FF_W), f32), pltpu.VMEM((2, TM, FF_W), bf16),
                        pltpu.VMEM((TM, D_MODEL), f32), pltpu.VMEM((TM, D_MODEL), bf16),
                        pltpu.VMEM((TM, D_MODEL), f32),
                        pltpu.VMEM((N_SLOTS, PAGES_PER_CHUNK, N_HEADS, HEAD_DIM, PAGE), f32),
                        pltpu.VMEM((2, n_pages, N_HEADS, PAGE), f32),
                        pltpu.VMEM((n_pages, N_HEADS, PAGE), f32),
                        pltpu.VMEM((N_HEADS, HEAD_DIM, PAGE), f32),
                        pltpu.SemaphoreType.DMA((N_SLOTS,)), pltpu.SemaphoreType.DMA((2,))])
    return pl.pallas_call(
        functools.partial(_ffn_paged_kernel, n_seq=n_seq, n_pages=n_pages),
        grid_spec=grid_spec,
        out_shape=[jax.ShapeDtypeStruct((nb, t, D_MODEL), f32),
                   jax.ShapeDtypeStruct((nb, SUBLANES, D_FF), f32),
                   jax.ShapeDtypeStruct((n_seq, N_HEADS, HEAD_DIM), f32)],
        compiler_params=pltpu.CompilerParams(dimension_semantics=("arbitrary", "arbitrary"),
                                             vmem_limit_bytes=VMEM_LIMIT),
        name="ffn_paged_attention",
    )(page_table, x_real, att, rn, natt, woa, wor, nffn, wup, fw, fb, wdn, nfin, gp0,
      qbc, q3, kn3, vn3, lfn3, ckT, cvT, clT)


def _meta_gp_kernel(x_ref, att_ref, rn_ref, natt_ref, woa_ref, wor_ref, nffn_ref, wup_ref, gpm_ref):
    an = _rms(att_ref[...], natt_ref[...]).astype(bf16)
    x1 = x_ref[...] + _dot(an, woa_ref[...]) + _dot(rn_ref[...].astype(bf16), wor_ref[...])
    hn = _rms(x1, nffn_ref[...]).astype(bf16)
    gpm_ref[...] = _dot(hn[0:2 * SUBLANES], wup_ref[...])[SUBLANES:2 * SUBLANES, :]


def _meta_gp_call(xs, att_m, rn_small, natt, woa, wor, nffn, wup):
    full = lambda shape: pl.BlockSpec(shape, lambda g: (0,) * len(shape))
    return pl.pallas_call(
        _meta_gp_kernel,
        grid=(1,),
        in_specs=[full((META_PAD, D_MODEL)), full((META_PAD, D_ATT)), full((META_PAD, D_RNN)), full((1, D_ATT)),
                  full((D_ATT, D_MODEL)), full((D_RNN, D_MODEL)), full((1, D_MODEL)), full((D_MODEL, D_FF))],
        out_specs=full((SUBLANES, D_FF)),
        out_shape=jax.ShapeDtypeStruct((SUBLANES, D_FF), f32),
        compiler_params=pltpu.CompilerParams(dimension_semantics=("arbitrary",), vmem_limit_bytes=VMEM_LIMIT),
        name="meta_ffn_gate",
    )(xs, att_m, rn_small, natt, woa, wor, nffn, wup)


def _block_diag(w):
    n, r, c = w.shape
    eye = jnp.eye(n, dtype=w.dtype)
    return (eye[:, None, :, None] * w[:, :, None, :]).reshape(n * r, n * c)


def kernel(x_prompt, x_sample, cache_k, cache_v, cache_logf, state_rnn_h, state_rnn_conv, state_ffn_conv, page_table, meta_tokens, norm_mix, w_in, b_forget, rnn_conv_w, rnn_conv_b, w_rg_a, b_rg_a, w_rg_x, b_rg_x, rg_lambda, norm_att_out, norm_rnn_out, w_out, norm_ffn, w_ffn_up, ffn_conv_w, ffn_conv_b, w_ffn_down, norm_final):
    nb, seq, _ = x_prompt.shape
    db = x_sample.shape[0]
    assert norm_mix.shape[0] == 1 and x_sample.shape[1] == 1 and seq % TM == 0 and seq % TQ == 0

    w = w_in[0]
    wf = jnp.pad(w[:, 3 * D_ATT:3 * D_ATT + N_HEADS], ((0, 0), (0, LANES - N_HEADS)))
    w_cat = jnp.concatenate([w[:, :D_ATT] * ATT_SCALE, w[:, D_ATT:3 * D_ATT], w[:, 3 * D_ATT + N_HEADS:], wf],
                            axis=1).astype(bf16)
    bda = _block_diag(w_rg_a[0])
    bdx = _block_diag(w_rg_x[0])
    wg = jnp.stack([jnp.concatenate([bda[:256, :256], bdx[:256, :256]], axis=1),
                    jnp.concatenate([bda[256:, 256:], bdx[256:, 256:]], axis=1)]).astype(bf16)
    woa = w_out[0, :D_ATT].astype(bf16)
    wor = w_out[0, D_ATT:].astype(bf16)
    wup = w_ffn_up[0].astype(bf16)
    wdn = w_ffn_down[0].astype(bf16)
    bf_col = b_forget[0][:, None]
    bf_row = b_forget
    nfin = norm_final[None, :]
    mix_args = (norm_mix, w_cat, bf_col)
    rnn_args = (rnn_conv_w[0], rnn_conv_b, wg, b_rg_a, b_rg_x, rg_lambda, norm_rnn_out)
    post_args = (norm_att_out, woa, wor, norm_ffn, wup, ffn_conv_w[0], ffn_conv_b, wdn, nfin)

    xs = jnp.concatenate([meta_tokens, jnp.zeros((META_PAD - N_META, D_MODEL), f32), x_sample[:, 0, :]], axis=0)
    (q_s, k_s, v_s, lf_s, h_s, xr_s, kTm, vTm, km, vm, lfTm, cTm, caug_m, att_m, h_m, xr_m, rn_small) = _small_pre_call(
        xs, *mix_args, bf_row, *rnn_args, state_rnn_h[0],
        state_rnn_conv[0, :, 0], state_rnn_conv[0, :, 1], state_rnn_conv[0, :, 2])

    gp_m = _meta_gp_call(xs, att_m, rn_small, norm_att_out, woa, wor, norm_ffn, wup)
    (q, kT, vT, kTb, vb, lfT, caug, rn, h_last, xr_last) = _inproj_call(
        x_prompt, *mix_args, *rnn_args, h_m, xr_m, cTm, km, vm)
    att = _attn_call(q, kTb, vb, caug, kTm.astype(bf16), vm.astype(bf16), caug_m)

    ckT = jnp.transpose(cache_k[0], (0, 2, 3, 1))
    cvT = jnp.transpose(cache_v[0], (0, 2, 3, 1))
    clT = jnp.transpose(cache_logf[0], (0, 2, 1))
    q3 = q_s.reshape(db, N_HEADS, HEAD_DIM)
    qbc = jnp.broadcast_to(q3[:, :, :, None], (db, N_HEADS, HEAD_DIM, PAGE))
    y_prompt, gp_last, att_s = _ffn_paged_call(
        page_table, x_prompt, att, rn, *post_args, gp_m, qbc, q3, k_s.reshape(db, N_HEADS, HEAD_DIM),
        v_s.reshape(db, N_HEADS, HEAD_DIM), lf_s[:, :, None], ckT, cvT, clT)

    att_small = jnp.concatenate([att_m, att_s.reshape(db, D_ATT)], axis=0)
    y_s, gp_s, _ = _small_post_call(xs, att_small, rn_small, *post_args,
                                    state_ffn_conv[0, :, 0], state_ffn_conv[0, :, 1])

    def with_meta(meta_t, real_t):
        c = meta_t.shape[0]
        return jnp.concatenate([jnp.broadcast_to(meta_t[None, :, :N_META], (nb, c, N_META)), real_t], axis=2)

    t_all = N_META + seq
    k_prompt = jnp.transpose(kT.reshape(nb, N_HEADS, HEAD_DIM, t_all), (0, 3, 1, 2))[None]
    v_prompt = jnp.transpose(vT.reshape(nb, N_HEADS, HEAD_DIM, t_all), (0, 3, 1, 2))[None]
    logf_prompt = jnp.transpose(with_meta(lfTm, lfT), (0, 2, 1))[None]
    rnn_h_prompt = h_last[:, 0, :][None]
    rnn_conv_prompt = xr_last[:, SUBLANES - 3:, :][None]
    ffn_conv_prompt = gp_last[:, SUBLANES - 2:, :][None]

    k_sample = k_s.reshape(1, db, 1, N_HEADS, HEAD_DIM)
    v_sample = v_s.reshape(1, db, 1, N_HEADS, HEAD_DIM)
    logf_sample = lf_s.reshape(1, db, 1, N_HEADS)
    rnn_h_sample = h_s[None]
    rnn_conv_sample = jnp.stack([state_rnn_conv[0, :, 1], state_rnn_conv[0, :, 2], xr_s], axis=1)[None]
    ffn_conv_sample = jnp.stack([state_ffn_conv[0, :, 1], gp_s], axis=1)[None]
    y_sample = y_s[:, None, :]

    return (y_prompt, y_sample, k_prompt, v_prompt, logf_prompt, rnn_h_prompt, rnn_conv_prompt, ffn_conv_prompt,
            k_sample, v_sample, logf_sample, rnn_h_sample, rnn_conv_sample, ffn_conv_sample)
```
